```python
import math, functools
import jax, jax.numpy as jnp
from jax import lax
import numpy as np

D_MODEL = 1024
BATCH = 2
SEQ = 8192
DEPTH = 4
DEC_BATCH = 128
DEC_SEQ = 8
PAST_LEN = 2048
PAGE_SIZE = 128

SSM_GROUPS = 16
SSM_GROUP_CH = 16
SSM_WIDTH = SSM_GROUPS * SSM_GROUP_CH
SSM_STATE = 64
SSM_DT_MIN = 1e-3
SSM_DT_MAX = 1e-1
ATT_HEADS = 4
ATT_DH = 64
ATT_VDIM = 2 * ATT_DH
ATT_QK_WIDTH = ATT_HEADS * 2 * ATT_DH
ATT_V_WIDTH = ATT_HEADS * ATT_VDIM
Q_BLOCK = 128
ROPE_THETA = 10000.0
MASK_VALUE = -1e30
HG_HEADS = 4
HG_DK = 64
HG_DV = 64
HG_WIDTH = HG_HEADS * HG_DK
HG_CHUNK = 64
HG_F_FLOOR = 1e-20
MEM_TOKENS = 256
MEM_HEADS = 4
MEM_DH = 64
MEM_WIDTH = MEM_HEADS * MEM_DH
N_BRANCH = 4
D_FF = -(-8 * D_MODEL // (3 * 256)) * 256
EPS = 1e-6
IN_SPLITS = (SSM_WIDTH, ATT_QK_WIDTH, ATT_QK_WIDTH, ATT_V_WIDTH, HG_WIDTH, HG_WIDTH, HG_WIDTH, HG_WIDTH, MEM_WIDTH, N_BRANCH * D_MODEL)
IN_WIDTH = sum(IN_SPLITS)

kernel_name = 'hybrid_s5_diffattn_hgrn2_step'


def rmsnorm(x, g):
    x32 = x.astype(jnp.float32)
    y = x32 * lax.rsqrt(jnp.mean(x32 * x32, axis=-1, keepdims=True) + EPS)
    return (y * g.astype(jnp.float32)).astype(x.dtype)


def rope(x, pos):
    dh = x.shape[-1]
    inv = ROPE_THETA ** (-jnp.arange(0, dh, 2, dtype=jnp.float32) / dh)
    ang = pos[:, None] * inv[None, :]
    ang = jnp.concatenate([ang, ang], axis=-1)
    shp = (pos.shape[0],) + (1,) * (x.ndim - 3) + (dh,)
    cos = jnp.cos(ang).reshape(shp)
    sin = jnp.sin(ang).reshape(shp)
    rot = jnp.concatenate([-x[..., dh // 2:], x[..., :dh // 2]], axis=-1)
    return (x * cos + rot * sin).astype(x.dtype)


def diff_core(q1, q2, k1, k2, v, mask, lam):
    scale = ATT_DH ** -0.5
    s1 = jnp.einsum('bqhd,bkhd->bhqk', q1, k1).astype(jnp.float32) * scale
    s2 = jnp.einsum('bqhd,bkhd->bhqk', q2, k2).astype(jnp.float32) * scale
    p1 = jax.nn.softmax(jnp.where(mask, s1, MASK_VALUE), axis=-1)
    p2 = jax.nn.softmax(jnp.where(mask, s2, MASK_VALUE), axis=-1)
    a = (p1 - lam * p2).astype(v.dtype)
    return jnp.einsum('bhqk,bkhe->bqhe', a, v)


def diff_attn_prompt(q1, q2, k1, k2, v, lam):
    b, t, h, d = q1.shape
    nb = t // Q_BLOCK
    kpos = jnp.arange(t)

    def to_blocks(q):
        return q.reshape(b, nb, Q_BLOCK, h, d).swapaxes(0, 1)

    def one(args):
        a1, a2, i = args
        qpos = i * Q_BLOCK + jnp.arange(Q_BLOCK)
        return diff_core(a1, a2, k1, k2, v, kpos[None, :] <= qpos[:, None], lam)

    o = lax.map(one, (to_blocks(q1), to_blocks(q2), jnp.arange(nb)))
    return o.swapaxes(0, 1).reshape(b, t, h, v.shape[-1])


def diff_attn_sample(q1, q2, k1, k2, v, lam, past_k, past_v):
    past = past_k.shape[1]
    t = q1.shape[1]
    kk1 = jnp.concatenate([past_k[..., :ATT_DH], k1], axis=1)
    kk2 = jnp.concatenate([past_k[..., ATT_DH:], k2], axis=1)
    vv = jnp.concatenate([past_v, v], axis=1)
    kpos = jnp.arange(past + t)
    qpos = past + jnp.arange(t)
    return diff_core(q1, q2, kk1, kk2, vv, kpos[None, :] <= qpos[:, None], lam)


def _scan_combine(e1, e2):
    a1r, a1i, b1r, b1i = e1
    a2r, a2i, b2r, b2i = e2
    return (a2r * a1r - a2i * a1i,
            a2r * a1i + a2i * a1r,
            a2r * b1r - a2i * b1i + b2r,
            a2r * b1i + a2i * b1r + b2i)


def s5_mixer(u, h0_re, h0_im, lam_re, lam_im, log_dt, b_re, b_im, c_re, c_im, d):
    f32 = jnp.float32
    dt = jnp.exp(log_dt.astype(f32))[:, None]
    lr, li = lam_re.astype(f32), lam_im.astype(f32)
    mag = jnp.exp(lr * dt)
    ar, ai = mag * jnp.cos(li * dt), mag * jnp.sin(li * dt)
    nr, ni = ar - 1.0, ai
    den = lr * lr + li * li
    cr, ci = (nr * lr + ni * li) / den, (ni * lr - nr * li) / den
    br, bi = b_re.astype(f32), b_im.astype(f32)
    bbr = cr[..., None] * br - ci[..., None] * bi
    bbi = cr[..., None] * bi + ci[..., None] * br
    bur = jnp.einsum('gpc,btgc->btgp', bbr, u)
    bui = jnp.einsum('gpc,btgc->btgp', bbi, u)
    bur = bur.at[:, 0].add(ar * h0_re - ai * h0_im)
    bui = bui.at[:, 0].add(ar * h0_im + ai * h0_re)
    _, _, xr, xi = lax.associative_scan(
        _scan_combine,
        (jnp.broadcast_to(ar, bur.shape), jnp.broadcast_to(ai, bur.shape), bur, bui),
        axis=1)
    y = (jnp.einsum('gcp,btgp->btgc', c_re.astype(f32), xr)
         - jnp.einsum('gcp,btgp->btgc', c_im.astype(f32), xi)
         + d.astype(f32) * u)
    return y, xr[:, -1], xi[:, -1]


def hgrn_chunked(q, logf, k, v, s0):
    b, t, h, dk = q.shape
    dv = v.shape[-1]
    c = math.gcd(t, HG_CHUNK)
    n = t // c
    causal = jnp.tril(jnp.ones((c, c), dtype=bool))[None, :, :, None, None]

    def chunks(a):
        return a.reshape(b, n, c, h, a.shape[-1]).swapaxes(0, 1)

    def step(s, inp):
        qc, lf, kc, vc = inp
        cum = jnp.cumsum(lf, axis=1)
        diff = cum[:, :, None] - cum[:, None, :]
        decay = jnp.where(causal, jnp.exp(jnp.where(causal, diff, 0.0)), 0.0)
        att = jnp.einsum('bthk,btshk,bshk->bhts', qc, decay, kc)
        o = (jnp.einsum('bhts,bshv->bthv', att, vc)
             + jnp.einsum('bthk,bhkv->bthv', qc * jnp.exp(cum), s))
        last = cum[:, -1:]
        s_new = (jnp.exp(last[:, 0])[..., None] * s
                 + jnp.einsum('bshk,bshv->bhkv', kc * jnp.exp(last - cum), vc))
        return s_new, o

    s_fin, o = lax.scan(step, s0, (chunks(q), chunks(logf), chunks(k), chunks(v)))
    return o.swapaxes(0, 1).reshape(b, t, h, dv), s_fin


def block(h, pos, p, lam_init, lb, att_fn, mem_k, mem_v, ssm_re0, ssm_im0, hg0):
    f32 = jnp.float32
    b, t, _ = h.shape
    xn = rmsnorm(h, p['norm_mix'])
    z = xn @ p['w_in']
    split_at = [int(i) for i in np.cumsum(IN_SPLITS)[:-1]]
    u, zq, zk, zv, hq, hf, hi, hg, mq, zg = jnp.split(z, split_at, axis=-1)

    y, ssm_re, ssm_im = s5_mixer(
        u.reshape(b, t, SSM_GROUPS, SSM_GROUP_CH).astype(f32), ssm_re0.astype(f32), ssm_im0.astype(f32),
        p['ssm_lam_re'], p['ssm_lam_im'], p['ssm_log_dt'], p['ssm_b_re'], p['ssm_b_im'],
        p['ssm_c_re'], p['ssm_c_im'], p['ssm_d'])
    ys = jax.nn.gelu(y.reshape(b, t, SSM_WIDTH)).astype(h.dtype)
    ys = ys * jax.nn.sigmoid(ys @ p['ssm_w_glu'])

    q = rope(zq.reshape(b, t, ATT_HEADS, 2, ATT_DH), pos)
    k = rope(zk.reshape(b, t, ATT_HEADS, 2, ATT_DH), pos)
    v = zv.reshape(b, t, ATT_HEADS, ATT_VDIM)
    lam = (jnp.exp(jnp.sum(p['att_lq1'].astype(f32) * p['att_lk1'].astype(f32)))
           - jnp.exp(jnp.sum(p['att_lq2'].astype(f32) * p['att_lk2'].astype(f32))) + lam_init)
    oa = att_fn(q[..., 0, :], q[..., 1, :], k[..., 0, :], k[..., 1, :], v, lam)
    oa = (rmsnorm(oa, p['att_norm']) * (1.0 - lam_init)).reshape(b, t, ATT_V_WIDTH)

    lb_h = lb.astype(f32).reshape(HG_HEADS, HG_DK)
    f = lb_h + (1.0 - lb_h) * jax.nn.sigmoid(hf.reshape(b, t, HG_HEADS, HG_DK).astype(f32))
    logf = jnp.log(jnp.maximum(f, HG_F_FLOOR))
    oh, hg_state = hgrn_chunked(
        jax.nn.silu(hq.reshape(b, t, HG_HEADS, HG_DK).astype(f32)), logf, 1.0 - f,
        hi.reshape(b, t, HG_HEADS, HG_DV).astype(f32), hg0.astype(f32))
    oh = (rmsnorm(oh, p['hg_norm']).astype(h.dtype)
          * jax.nn.silu(hg.reshape(b, t, HG_HEADS, HG_DV))).reshape(b, t, HG_WIDTH)

    qm = mq.reshape(b, t, MEM_HEADS, MEM_DH)
    sm = jnp.einsum('bqhd,bkhd->bhqk', qm, mem_k).astype(f32) * (MEM_DH ** -0.5)
    pm = jax.nn.softmax(sm, axis=-1).astype(mem_v.dtype)
    om = jnp.einsum('bhqk,bkhd->bqhd', pm, mem_v).reshape(b, t, MEM_WIDTH)

    g = jax.nn.sigmoid(zg.reshape(b, t, N_BRANCH, D_MODEL))
    merged = (g[:, :, 0] * (ys @ p['wb_ssm']) + g[:, :, 1] * (oa @ p['wb_att'])
              + g[:, :, 2] * (oh @ p['wb_hg']) + g[:, :, 3] * (om @ p['wb_mem']))
    h = h + (merged @ p['w_out']).astype(h.dtype)

    hn = rmsnorm(h, p['norm_ffn'])
    f_gate, f_up = jnp.split(hn @ p['w_ffn_in'], 2, axis=-1)
    h = h + ((jax.nn.silu(f_gate) * f_up) @ p['w_ffn_out']).astype(h.dtype)
    return h, k.reshape(b, t, ATT_HEADS, ATT_VDIM), v, ssm_re, ssm_im, hg_state


def setup_inputs(seed: int = 0) -> dict:
    key = jax.random.key(seed)
    keys = jax.random.split(key, 64)
    cnt = [0]
    f32 = jnp.float32

    def nk():
        cnt[0] += 1
        return keys[cnt[0] - 1]

    def nrm(shape, scale=1.0):
        return jax.random.normal(nk(), shape, f32) * scale

    def gain(shape):
        return 1.0 + nrm(shape, 0.02)

    n_pages = PAST_LEN // PAGE_SIZE
    n_used = DEC_BATCH * n_pages
    n_phys = n_used + max(n_used // 4, 1)
    x_prompt = nrm((BATCH, SEQ, D_MODEL))
    x_sample = nrm((DEC_BATCH, DEC_SEQ, D_MODEL))
    mem_prompt = nrm((BATCH, MEM_TOKENS, D_MODEL))
    cache_k = nrm((DEPTH, n_phys, PAGE_SIZE, ATT_HEADS, ATT_VDIM))
    cache_v = nrm((DEPTH, n_phys, PAGE_SIZE, ATT_HEADS, ATT_VDIM))
    page_table = jax.random.permutation(nk(), n_phys)[:n_used].reshape(DEC_BATCH, n_pages).astype(jnp.int32)
    cache_mem_k = nrm((DEPTH, DEC_BATCH, MEM_TOKENS, MEM_HEADS, MEM_DH))
    cache_mem_v = nrm((DEPTH, DEC_BATCH, MEM_TOKENS, MEM_HEADS, MEM_DH))
    state_ssm_re = nrm((DEPTH, DEC_BATCH, SSM_GROUPS, SSM_STATE), 0.5)
    state_ssm_im = nrm((DEPTH, DEC_BATCH, SSM_GROUPS, SSM_STATE), 0.5)
    state_hgrn = nrm((DEPTH, DEC_BATCH, HG_HEADS, HG_DK, HG_DV), 0.5)
    return {
        'x_prompt': x_prompt,
        'x_sample': x_sample,
        'mem_prompt': mem_prompt,
        'cache_k': cache_k,
        'cache_v': cache_v,
        'page_table': page_table,
        'cache_mem_k': cache_mem_k,
        'cache_mem_v': cache_mem_v,
        'state_ssm_re': state_ssm_re,
        'state_ssm_im': state_ssm_im,
        'state_hgrn': state_hgrn,
        'norm_mix': gain((DEPTH, D_MODEL)),
        'w_in': nrm((DEPTH, D_MODEL, IN_WIDTH), D_MODEL ** -0.5),
        'ssm_lam_re': -0.5 + nrm((DEPTH, SSM_GROUPS, SSM_STATE), 0.01),
        'ssm_lam_im': jnp.pi * jnp.arange(SSM_STATE, dtype=f32) + nrm((DEPTH, SSM_GROUPS, SSM_STATE), 0.01),
        'ssm_log_dt': jax.random.uniform(nk(), (DEPTH, SSM_GROUPS), f32, math.log(SSM_DT_MIN), math.log(SSM_DT_MAX)),
        'ssm_b_re': nrm((DEPTH, SSM_GROUPS, SSM_STATE, SSM_GROUP_CH), (0.5 / SSM_GROUP_CH) ** 0.5),
        'ssm_b_im': nrm((DEPTH, SSM_GROUPS, SSM_STATE, SSM_GROUP_CH), (0.5 / SSM_GROUP_CH) ** 0.5),
        'ssm_c_re': nrm((DEPTH, SSM_GROUPS, SSM_GROUP_CH, SSM_STATE), (0.5 / SSM_STATE) ** 0.5),
        'ssm_c_im': nrm((DEPTH, SSM_GROUPS, SSM_GROUP_CH, SSM_STATE), (0.5 / SSM_STATE) ** 0.5),
        'ssm_d': nrm((DEPTH, SSM_GROUPS, SSM_GROUP_CH)),
        'ssm_w_glu': nrm((DEPTH, SSM_WIDTH, SSM_WIDTH), SSM_WIDTH ** -0.5),
        'att_lq1': nrm((DEPTH, ATT_DH), 0.1),
        'att_lk1': nrm((DEPTH, ATT_DH), 0.1),
        'att_lq2': nrm((DEPTH, ATT_DH), 0.1),
        'att_lk2': nrm((DEPTH, ATT_DH), 0.1),
        'att_norm': gain((DEPTH, ATT_VDIM)),
        'hg_gamma': nrm((DEPTH, HG_WIDTH), 0.5),
        'hg_norm': gain((DEPTH, HG_DV)),
        'mem_norm': gain((DEPTH, D_MODEL)),
        'w_mem_kv': nrm((DEPTH, D_MODEL, 2 * MEM_WIDTH), D_MODEL ** -0.5),
        'wb_ssm': nrm((DEPTH, SSM_WIDTH, D_MODEL), SSM_WIDTH ** -0.5),
        'wb_att': nrm((DEPTH, ATT_V_WIDTH, D_MODEL), ATT_V_WIDTH ** -0.5),
        'wb_hg': nrm((DEPTH, HG_WIDTH, D_MODEL), HG_WIDTH ** -0.5),
        'wb_mem': nrm((DEPTH, MEM_WIDTH, D_MODEL), MEM_WIDTH ** -0.5),
        'w_out': nrm((DEPTH, D_MODEL, D_MODEL), D_MODEL ** -0.5),
        'norm_ffn': gain((DEPTH, D_MODEL)),
        'w_ffn_in': nrm((DEPTH, D_MODEL, 2 * D_FF), D_MODEL ** -0.5),
        'w_ffn_out': nrm((DEPTH, D_FF, D_MODEL), D_FF ** -0.5),
        'norm_final': gain((D_MODEL,)),
    }


def reference(x_prompt, x_sample, mem_prompt, cache_k, cache_v, page_table, cache_mem_k, cache_mem_v,
              state_ssm_re, state_ssm_im, state_hgrn, norm_mix, w_in, ssm_lam_re, ssm_lam_im, ssm_log_dt,
              ssm_b_re, ssm_b_im, ssm_c_re, ssm_c_im, ssm_d, ssm_w_glu, att_lq1, att_lk1, att_lq2, att_lk2,
              att_norm, hg_gamma, hg_norm, mem_norm, w_mem_kv, wb_ssm, wb_att, wb_hg, wb_mem, w_out,
              norm_ffn, w_ffn_in, w_ffn_out, norm_final):
    f32 = jnp.float32
    gam = jax.nn.softmax(hg_gamma.astype(f32), axis=0)
    lb_all = jnp.cumsum(gam, axis=0) - gam[0]

    def layer_params(l):
        return {
            'norm_mix': norm_mix[l], 'w_in': w_in[l],
            'ssm_lam_re': ssm_lam_re[l], 'ssm_lam_im': ssm_lam_im[l], 'ssm_log_dt': ssm_log_dt[l],
            'ssm_b_re': ssm_b_re[l], 'ssm_b_im': ssm_b_im[l], 'ssm_c_re': ssm_c_re[l], 'ssm_c_im': ssm_c_im[l],
            'ssm_d': ssm_d[l], 'ssm_w_glu': ssm_w_glu[l],
            'att_lq1': att_lq1[l], 'att_lk1': att_lk1[l], 'att_lq2': att_lq2[l], 'att_lk2': att_lk2[l],
            'att_norm': att_norm[l], 'hg_norm': hg_norm[l],
            'wb_ssm': wb_ssm[l], 'wb_att': wb_att[l], 'wb_hg': wb_hg[l], 'wb_mem': wb_mem[l],
            'w_out': w_out[l], 'norm_ffn': norm_ffn[l], 'w_ffn_in': w_ffn_in[l], 'w_ffn_out': w_ffn_out[l],
        }

    b, s, _ = x_prompt.shape
    db, ds, _ = x_sample.shape
    n_pages = page_table.shape[1]
    past = n_pages * cache_k.shape[2]
    pos_p = jnp.arange(s, dtype=f32)
    pos_s = past + jnp.arange(ds, dtype=f32)
    hp, hs = x_prompt, x_sample
    kp, vp, mkp, mvp, srp, sip, hgp = [], [], [], [], [], [], []
    kss, vss, srs, sis, hgs = [], [], [], [], []
    for l in range(DEPTH):
        p = layer_params(l)
        lam_init = 0.8 - 0.6 * math.exp(-0.3 * l)
        mkv = rmsnorm(mem_prompt, mem_norm[l]) @ w_mem_kv[l]
        mk = mkv[..., :MEM_WIDTH].reshape(b, -1, MEM_HEADS, MEM_DH)
        mv = mkv[..., MEM_WIDTH:].reshape(b, -1, MEM_HEADS, MEM_DH)
        hp, kr, vr, sr, si, sh = block(
            hp, pos_p, p, lam_init, lb_all[l], diff_attn_prompt, mk, mv,
            jnp.zeros((b, SSM_GROUPS, SSM_STATE), f32), jnp.zeros((b, SSM_GROUPS, SSM_STATE), f32),
            jnp.zeros((b, HG_HEADS, HG_DK, HG_DV), f32))
        kp.append(kr); vp.append(vr); mkp.append(mk); mvp.append(mv)
        srp.append(sr); sip.append(si); hgp.append(sh)
        past_k = cache_k[l, page_table].reshape(db, past, ATT_HEADS, ATT_VDIM)
        past_v = cache_v[l, page_table].reshape(db, past, ATT_HEADS, ATT_VDIM)
        att = functools.partial(diff_attn_sample, past_k=past_k, past_v=past_v)
        hs, kr, vr, sr, si, sh = block(
            hs, pos_s, p, lam_init, lb_all[l], att, cache_mem_k[l], cache_mem_v[l],
            state_ssm_re[l], state_ssm_im[l], state_hgrn[l])
        kss.append(kr); vss.append(vr); srs.append(sr); sis.append(si); hgs.append(sh)
    y_prompt = rmsnorm(hp, norm_final)
    y_sample = rmsnorm(hs, norm_final)
    st = jnp.stack
    return (y_prompt, y_sample, st(kp), st(vp), st(mkp), st(mvp), st(srp), st(sip), st(hgp),
            st(kss), st(vss), st(srs), st(sis), st(hgs))
```

```python
import functools
import math

import jax
import jax.numpy as jnp
from jax import lax
from jax.experimental import pallas as pl
from jax.experimental.pallas import tpu as pltpu

F32 = jnp.float32
BF16 = jnp.bfloat16
SDS = jax.ShapeDtypeStruct

D_MODEL = 1024
EPS = 1e-6
ROPE_THETA = 10000.0
MASK_VALUE = -1e30
HG_F_FLOOR = 1e-20
N_HEADS = 4
HEAD_LANES = 128
MEM_TOKENS = 256
D_FF = 2816
IN_WIDTH = 7168
IN_TILE = 512
CHUNK = 8
VMEM_LIMIT = 56 * 1024 * 1024

COL_HQ, COL_HF, COL_HI, COL_HG, COL_MQ, COL_U = 6, 7, 8, 9, 10, 11
GATE_COL0 = 3


def _cparams(n_axes):
    return pltpu.CompilerParams(dimension_semantics=("arbitrary",) * n_axes,
                                vmem_limit_bytes=VMEM_LIMIT)


def _sigmoid(x):
    return 1.0 / (1.0 + jnp.exp(-x))


def _silu(x):
    return x * _sigmoid(x)


def _rms(x, g):
    return x * lax.rsqrt(jnp.mean(x * x, axis=-1, keepdims=True) + EPS) * g


def _dot(a, b):
    return jnp.dot(a, b, preferred_element_type=F32)


def _dot_nt(a, b):
    return lax.dot_general(a, b, (((1,), (1,)), ((), ())), preferred_element_type=F32)


def _block_ones(n, blk, dtype):
    r = lax.broadcasted_iota(jnp.int32, (n, n), 0) // blk
    c = lax.broadcasted_iota(jnp.int32, (n, n), 1) // blk
    return r == c


def _inproj_kernel(x_ref, g_ref, w_ref, cos_ref, sa_ref, sb_ref, z_ref, zb_ref, xn_ref):
    j = pl.program_id(1)

    @pl.when(j == 0)
    def _():
        xn_ref[...] = _rms(x_ref[...], g_ref[...]).astype(BF16)

    acc = _dot(xn_ref[...], w_ref[...])

    @pl.when(j < 2)
    def _():
        cos, sa, sb = cos_ref[...], sa_ref[...], sb_ref[...]
        for hh in range(N_HEADS):
            sl = slice(hh * HEAD_LANES, (hh + 1) * HEAD_LANES)
            x = acc[:, sl]
            r = (x * cos + pltpu.roll(x, 32, 1) * sa + pltpu.roll(x, 96, 1) * sb)
            z_ref[:, sl] = r
            zb_ref[:, sl] = r.astype(BF16)

    @pl.when(j == 2)
    def _():
        z_ref[...] = acc
        zb_ref[...] = acc.astype(BF16)

    @pl.when(jnp.logical_and(j > 2, j < 6))
    def _():
        z_ref[...] = acc

    @pl.when(j >= 6)
    def _():
        z_ref[...] = _sigmoid(acc)


def in_proj(h, g, w, cos_t, sa_t, sb_t, tm):
    n = h.shape[0]
    nper = cos_t.shape[0] // tm
    tab = pl.BlockSpec((tm, HEAD_LANES), lambda i, j: (i % nper, 0))
    return pl.pallas_call(
        _inproj_kernel,
        grid=(n // tm, IN_WIDTH // IN_TILE),
        in_specs=[pl.BlockSpec((tm, D_MODEL), lambda i, j: (i, 0)),
                  pl.BlockSpec((1, D_MODEL), lambda i, j: (0, 0)),
                  pl.BlockSpec((D_MODEL, IN_TILE), lambda i, j: (0, j)),
                  tab, tab, tab],
        out_specs=[pl.BlockSpec((tm, IN_TILE), lambda i, j: (i, j)),
                   pl.BlockSpec((tm, IN_TILE), lambda i, j: (i, jnp.minimum(j, 2)))],
        out_shape=[SDS((n, IN_WIDTH), F32), SDS((n, 3 * IN_TILE), BF16)],
        scratch_shapes=[pltpu.VMEM((tm, D_MODEL), BF16)],
        compiler_params=_cparams(2),
        name="in_proj",
    )(h, g, w, cos_t, sa_t, sb_t)


def _memkv_kernel(x_ref, g_ref, w_ref, o_ref):
    o_ref[...] = _dot(_rms(x_ref[...], g_ref[...]).astype(BF16), w_ref[...])


def mem_kv_proj(x, g, w):
    n, wd = x.shape[0], w.shape[1]
    tm = 256
    return pl.pallas_call(
        _memkv_kernel,
        grid=(n // tm,),
        in_specs=[pl.BlockSpec((tm, D_MODEL), lambda i: (i, 0)),
                  pl.BlockSpec((1, D_MODEL), lambda i: (0, 0)),
                  pl.BlockSpec((D_MODEL, wd), lambda i: (0, 0))],
        out_specs=pl.BlockSpec((tm, wd), lambda i: (i, 0)),
        out_shape=SDS((n, wd), F32),
        compiler_params=_cparams(1),
        name="mem_kv_proj",
    )(x, g, w)


def _s5_kernel(u_ref, h0r_ref, h0i_ref, bre_ref, bim_ref, cre_ref, cim_ref, d_ref, wg_ref,
               pw_ref, ys_ref, sr_ref, si_ref, xr_sc, xi_sc, hr_sc, hi_sc):
    t = pl.program_id(1)
    tt = u_ref.shape[0]
    u = u_ref[...]
    ub = u.astype(BF16)
    xr_sc[...] = _dot(ub, bre_ref[...])
    xi_sc[...] = _dot(ub, bim_ref[...])

    @pl.when(t == 0)
    def _():
        hr_sc[...] = jnp.broadcast_to(h0r_ref[...], hr_sc.shape)
        hi_sc[...] = jnp.broadcast_to(h0i_ref[...], hi_sc.shape)

    row = lax.broadcasted_iota(jnp.int32, (CHUNK, 1), 0)
    apr, api = pw_ref[6], pw_ref[7]

    def body(c, carry):
        hr, hi = carry
        sl = pl.ds(pl.multiple_of(c * CHUNK, CHUNK), CHUNK)
        xr, xi = xr_sc[sl, :], xi_sc[sl, :]
        for k, s in enumerate((1, 2, 4)):
            ar, ai = pw_ref[2 * k], pw_ref[2 * k + 1]
            keep = row >= s
            pr = jnp.where(keep, pltpu.roll(xr, s, 0), 0.0)
            pi = jnp.where(keep, pltpu.roll(xi, s, 0), 0.0)
            xr, xi = xr + (ar * pr - ai * pi), xi + (ar * pi + ai * pr)
        xr, xi = xr + (apr * hr - api * hi), xi + (apr * hi + api * hr)
        xr_sc[sl, :] = xr
        xi_sc[sl, :] = xi
        return (jnp.broadcast_to(xr[CHUNK - 1:CHUNK, :], xr.shape),
                jnp.broadcast_to(xi[CHUNK - 1:CHUNK, :], xi.shape))

    hr, hi = lax.fori_loop(0, tt // CHUNK, body, (hr_sc[...], hi_sc[...]))
    hr_sc[...] = hr
    hi_sc[...] = hi
    sr_ref[...] = hr[0:1, :]
    si_ref[...] = hi[0:1, :]

    y = (_dot(xr_sc[...].astype(BF16), cre_ref[...]) - _dot(xi_sc[...].astype(BF16), cim_ref[...])
         + d_ref[...] * u)
    ys = jax.nn.gelu(y)
    gate = _sigmoid(_dot(ys.astype(BF16), wg_ref[...]))
    ys_ref[...] = (ys * gate).astype(BF16)


def s5_mixer(z, h0r, h0i, prm, wglu, nb, t_len, tt):
    nt = t_len // tt
    ns = h0r.shape[-1]
    width = prm["d"].shape[-1]
    full = lambda a: pl.BlockSpec(a.shape, lambda b, t: (0,) * a.ndim)
    st = pl.BlockSpec((None, 1, ns), lambda b, t: (b, 0, 0))
    return pl.pallas_call(
        _s5_kernel,
        grid=(nb, nt),
        in_specs=[pl.BlockSpec((tt, width), lambda b, t: (b * nt + t, COL_U)),
                  st, st,
                  full(prm["bre"]), full(prm["bim"]), full(prm["cre"]), full(prm["cim"]),
                  full(prm["d"]), full(wglu), full(prm["pw"])],
        out_specs=[pl.BlockSpec((tt, width), lambda b, t: (b * nt + t, 0)), st, st],
        out_shape=[SDS((nb * t_len, width), BF16), SDS((nb, 1, ns), F32), SDS((nb, 1, ns), F32)],
        scratch_shapes=[pltpu.VMEM((tt, ns), F32), pltpu.VMEM((tt, ns), F32),
                        pltpu.VMEM((CHUNK, ns), F32), pltpu.VMEM((CHUNK, ns), F32)],
        compiler_params=_cparams(2),
        name="s5",
    )(z, h0r, h0i, prm["bre"], prm["bim"], prm["cre"], prm["cim"], prm["d"], wglu, prm["pw"])


def _lambda(lq1, lk1, lq2, lk2, lam_init):
    return (jnp.exp(jnp.sum(lq1[...] * lk1[...], axis=-1, keepdims=True))
            - jnp.exp(jnp.sum(lq2[...] * lk2[...], axis=-1, keepdims=True)) + lam_init)


def _two_maps(q):
    lane = lax.broadcasted_iota(jnp.int32, (1, HEAD_LANES), 1)
    zero = jnp.zeros_like(q)
    qs = q * jnp.asarray(0.125, q.dtype)
    return jnp.concatenate([jnp.where(lane < 64, qs, zero), jnp.where(lane >= 64, qs, zero)], axis=0)


def _diff_finish(acc, l, lam, lam_init, gn):
    t = acc.shape[0] // 2
    o = acc / l
    o = o[:t] - lam * o[t:]
    return _rms(o, gn) * (1.0 - lam_init)


def _attn_prompt_kernel(cst_ref, q_ref, k_ref, v_ref, lq1, lk1, lq2, lk2, gn_ref, o_ref,
                        m_sc, l_sc, acc_sc):
    qi = pl.program_id(2)
    tq = q_ref.shape[0]
    lam_init = cst_ref[0]
    qq = _two_maps(q_ref[...])
    m_sc[...] = jnp.full(m_sc.shape, MASK_VALUE, F32)
    l_sc[...] = jnp.zeros(l_sc.shape, F32)
    acc_sc[...] = jnp.zeros(acc_sc.shape, F32)

    def step(kb, masked):
        sl = pl.ds(pl.multiple_of(kb * tq, tq), tq)
        s = _dot_nt(qq, k_ref[sl, :])
        if masked:
            r = lax.broadcasted_iota(jnp.int32, (2 * tq, tq), 0) % tq
            c = lax.broadcasted_iota(jnp.int32, (2 * tq, tq), 1)
            s = jnp.where(c <= r, s, MASK_VALUE)
        m_prev = m_sc[...]
        m_new = jnp.maximum(m_prev, jnp.max(s, axis=-1, keepdims=True))
        p = jnp.exp(s - m_new)
        alpha = jnp.exp(m_prev - m_new)
        l_sc[...] = alpha * l_sc[...] + jnp.sum(p, axis=-1, keepdims=True)
        acc_sc[...] = alpha * acc_sc[...] + _dot(p.astype(BF16), v_ref[sl, :])
        m_sc[...] = m_new

    def body(kb, carry):
        step(kb, False)
        return carry

    lax.fori_loop(0, qi, body, 0)
    step(qi, True)
    lam = _lambda(lq1, lk1, lq2, lk2, lam_init)
    o_ref[...] = _diff_finish(acc_sc[...], l_sc[...], lam, lam_init, gn_ref[...]).astype(BF16)


def attn_prompt(cst, zb, lq1, lk1, lq2, lk2, gn, nb, t_len, tq):
    nq = t_len // tq
    vec = pl.BlockSpec((1, 64), lambda b, h, i: (0, 0))
    return pl.pallas_call(
        _attn_prompt_kernel,
        grid=(nb, N_HEADS, nq),
        in_specs=[pl.BlockSpec(memory_space=pltpu.SMEM),
                  pl.BlockSpec((tq, HEAD_LANES), lambda b, h, i: (b * nq + i, h)),
                  pl.BlockSpec((t_len, HEAD_LANES), lambda b, h, i: (b, N_HEADS + h)),
                  pl.BlockSpec((t_len, HEAD_LANES), lambda b, h, i: (b, 2 * N_HEADS + h)),
                  vec, vec, vec, vec,
                  pl.BlockSpec((1, HEAD_LANES), lambda b, h, i: (0, 0))],
        out_specs=pl.BlockSpec((tq, HEAD_LANES), lambda b, h, i: (b * nq + i, h)),
        out_shape=SDS((nb * t_len, N_HEADS * HEAD_LANES), BF16),
        scratch_shapes=[pltpu.VMEM((2 * tq, 1), F32), pltpu.VMEM((2 * tq, 1), F32),
                        pltpu.VMEM((2 * tq, HEAD_LANES), F32)],
        compiler_params=_cparams(3),
        name="attn_prompt",
    )(cst, zb, zb, zb, lq1, lk1, lq2, lk2, gn)


def _attn_sample_kernel(pt_ref, lyr_ref, cst_ref, q_ref, kn_ref, vn_ref, lq1, lk1, lq2, lk2, gn_ref,
                        *rest):
    n_pages = (len(rest) - 1) // 2
    k_refs, v_refs, o_ref = rest[:n_pages], rest[n_pages:2 * n_pages], rest[-1]
    t = q_ref.shape[0]
    lam_init = cst_ref[0]
    lam = _lambda(lq1, lk1, lq2, lk2, lam_init)
    r = lax.broadcasted_iota(jnp.int32, (2 * t, t), 0) % t
    c = lax.broadcasted_iota(jnp.int32, (2 * t, t), 1)
    for hh in range(N_HEADS):
        sl = slice(hh * HEAD_LANES, (hh + 1) * HEAD_LANES)
        qq = _two_maps(q_ref[:, sl])
        s_past = [_dot_nt(qq, k_refs[p][:, sl].astype(BF16)) for p in range(n_pages)]
        s_new = jnp.where(c <= r, _dot_nt(qq, kn_ref[:, sl]), MASK_VALUE)
        m = s_past[0]
        for sp in s_past[1:]:
            m = jnp.maximum(m, sp)
        m = jnp.maximum(jnp.max(m, axis=-1, keepdims=True), jnp.max(s_new, axis=-1, keepdims=True))
        p_new = jnp.exp(s_new - m)
        l = jnp.sum(p_new, axis=-1, keepdims=True)
        acc = _dot(p_new.astype(BF16), vn_ref[:, sl])
        lsum = jnp.zeros(s_past[0].shape, F32)
        for p in range(n_pages):
            pp = jnp.exp(s_past[p] - m)
            lsum = lsum + pp
            acc = acc + _dot(pp.astype(BF16), v_refs[p][:, sl].astype(BF16))
        l = l + jnp.sum(lsum, axis=-1, keepdims=True)
        o_ref[:, sl] = _diff_finish(acc, l, lam, lam_init, gn_ref[...]).astype(BF16)


def attn_sample(page_table, lyr, cst, zb, cache_k, cache_v, lq1, lk1, lq2, lk2, gn, t_len):
    nb, n_pages = page_table.shape
    page = cache_k.shape[2]
    width = N_HEADS * HEAD_LANES
    pt = page_table.reshape(-1)
    vec = pl.BlockSpec((1, 64), lambda b, pt, ly: (0, 0))

    def page_spec(p):
        return pl.BlockSpec((None, None, page, width),
                            lambda b, pt, ly: (ly[0], pt[b * n_pages + p], 0, 0))

    pages = [page_spec(p) for p in range(n_pages)]
    grid_spec = pltpu.PrefetchScalarGridSpec(
        num_scalar_prefetch=2,
        grid=(nb,),
        in_specs=[pl.BlockSpec(memory_space=pltpu.SMEM),
                  pl.BlockSpec((t_len, width), lambda b, pt, ly: (b, 0)),
                  pl.BlockSpec((t_len, width), lambda b, pt, ly: (b, 1)),
                  pl.BlockSpec((t_len, width), lambda b, pt, ly: (b, 2)),
                  vec, vec, vec, vec,
                  pl.BlockSpec((1, HEAD_LANES), lambda b, pt, ly: (0, 0))] + pages + pages,
        out_specs=pl.BlockSpec((t_len, width), lambda b, pt, ly: (b, 0)),
    )
    return pl.pallas_call(
        _attn_sample_kernel,
        grid_spec=grid_spec,
        out_shape=SDS((nb * t_len, width), BF16),
        compiler_params=_cparams(1),
        name="attn_sample",
    )(pt, lyr, cst, zb, zb, zb, lq1, lk1, lq2, lk2, gn,
      *([cache_k] * n_pages), *([cache_v] * n_pages))


def _hgrn_kernel(hq_ref, hf_ref, hi_ref, hg_ref, lb_ref, gn_ref, s0_ref, oh_ref, sout_ref,
                 s_sc, q_sc, k_sc, lf_sc, o_sc):
    t = pl.program_id(1)
    nt = pl.num_programs(1)
    tt, w = hq_ref.shape
    dk = w // N_HEADS
    blk = _block_ones(w, dk, F32)
    ones_bf = jnp.where(blk, 1.0, 0.0).astype(BF16)

    @pl.when(t == 0)
    def _():
        s0 = s0_ref[...]
        s_sc[...] = jnp.where(blk, jnp.concatenate([s0] * N_HEADS, axis=1), 0.0)

    lb = lb_ref[...]
    f = lb + (1.0 - lb) * _sigmoid(hf_ref[...])
    lf_sc[...] = jnp.log(jnp.maximum(f, HG_F_FLOOR))
    k_sc[...] = 1.0 - f
    q_sc[...] = _silu(hq_ref[...])
    row = lax.broadcasted_iota(jnp.int32, (CHUNK, 1), 0)

    def body(c, carry):
        sl = pl.ds(pl.multiple_of(c * CHUNK, CHUNK), CHUNK)
        q8, k8, v8 = q_sc[sl, :], k_sc[sl, :], hi_ref[sl, :]
        cum = lf_sc[sl, :]
        for s in (1, 2, 4):
            cum = cum + jnp.where(row >= s, pltpu.roll(cum, s, 0), 0.0)
        prods = []
        for s in range(CHUNK):
            keep = row >= s
            e = jnp.where(keep, jnp.exp(jnp.where(keep, cum - cum[s:s + 1, :], 0.0)), 0.0)
            prods.append(q8 * e * k8[s:s + 1, :])
        att = _dot(jnp.concatenate(prods, axis=0).astype(BF16), ones_bf)
        o = att[0:CHUNK, :] * v8[0:1, :]
        for s in range(1, CHUNK):
            o = o + att[s * CHUNK:(s + 1) * CHUNK, :] * v8[s:s + 1, :]
        s_old = s_sc[...]
        o = o + _dot_nt((q8 * jnp.exp(cum)).astype(BF16), s_old.astype(BF16))
        o_sc[sl, :] = o
        last = cum[CHUNK - 1:CHUNK, :]
        kd = (k8 * jnp.exp(last - cum)).astype(BF16).astype(F32)
        vb = v8.astype(BF16).astype(F32)
        upd = lax.dot_general(vb, kd, (((0,), (0,)), ((), ())), preferred_element_type=F32)
        s_sc[...] = jnp.exp(last) * s_old + jnp.where(blk, upd, 0.0)
        return carry

    lax.fori_loop(0, tt // CHUNK, body, 0)

    o = o_sc[...]
    o2 = o * o
    hi = o2.astype(BF16)
    lo = (o2 - hi.astype(F32)).astype(BF16)
    ms = (_dot(hi, ones_bf) + _dot(lo, ones_bf)) * (1.0 / dk)
    on = o * lax.rsqrt(ms + EPS) * gn_ref[...]
    oh_ref[...] = (on * _silu(hg_ref[...])).astype(BF16)

    @pl.when(t == nt - 1)
    def _():
        sf = s_sc[...]
        acc = sf[:, 0:dk]
        for hh in range(1, N_HEADS):
            acc = acc + sf[:, hh * dk:(hh + 1) * dk]
        sout_ref[...] = acc


def hgrn_mixer(z, lb, gn, s0, nb, t_len, tt):
    nt = t_len // tt
    w = lb.shape[-1]
    dk = w // N_HEADS
    col = lambda cidx: pl.BlockSpec((tt, w), lambda b, t: (b * nt + t, cidx))
    row = pl.BlockSpec((1, w), lambda b, t: (0, 0))
    st = pl.BlockSpec((None, w, dk), lambda b, t: (b, 0, 0))
    return pl.pallas_call(
        _hgrn_kernel,
        grid=(nb, nt),
        in_specs=[col(COL_HQ), col(COL_HF), col(COL_HI), col(COL_HG), row, row, st],
        out_specs=[pl.BlockSpec((tt, w), lambda b, t: (b * nt + t, 0)), st],
        out_shape=[SDS((nb * t_len, w), BF16), SDS((nb, w, dk), F32)],
        scratch_shapes=[pltpu.VMEM((w, w), F32), pltpu.VMEM((tt, w), F32), pltpu.VMEM((tt, w), F32),
                        pltpu.VMEM((tt, w), F32), pltpu.VMEM((tt, w), F32)],
        compiler_params=_cparams(2),
        name="hgrn",
    )(z, z, z, z, lb, gn, s0)


def _mem_attn_kernel(off_ref, q_ref, mk_ref, mv_ref, o_ref):
    tq, w = q_ref.shape
    dh = w // N_HEADS
    q = q_ref[...].astype(BF16) * jnp.asarray(dh ** -0.5, BF16)
    mk = mk_ref[...].astype(BF16)
    mv = mv_ref[...].astype(BF16)
    head = lax.broadcasted_iota(jnp.int32, (1, w), 1) // dh
    out = jnp.zeros((tq, w), F32)
    for hh in range(N_HEADS):
        s = _dot_nt(jnp.where(head == hh, q, jnp.zeros_like(q)), mk)
        p = jnp.exp(s - jnp.max(s, axis=-1, keepdims=True))
        p = p / jnp.sum(p, axis=-1, keepdims=True)
        out = out + jnp.where(head == hh, _dot(p.astype(BF16), mv), 0.0)
    o_ref[...] = out.astype(BF16)


def mem_attn(off, z, mk, mv, k_col, v_col, nb, t_len, tq):
    nq = t_len // tq
    w = MEM_TOKENS
    grid_spec = pltpu.PrefetchScalarGridSpec(
        num_scalar_prefetch=1,
        grid=(nb, nq),
        in_specs=[pl.BlockSpec((tq, w), lambda b, i, off: (b * nq + i, COL_MQ)),
                  pl.BlockSpec((MEM_TOKENS, w), lambda b, i, off: (off[0] + b, k_col)),
                  pl.BlockSpec((MEM_TOKENS, w), lambda b, i, off: (off[0] + b, v_col))],
        out_specs=pl.BlockSpec((tq, w), lambda b, i, off: (b * nq + i, 0)),
    )
    return pl.pallas_call(
        _mem_attn_kernel,
        grid_spec=grid_spec,
        out_shape=SDS((nb * t_len, w), BF16),
        compiler_params=_cparams(2),
        name="mem_attn",
    )(off, z, mk, mv)


def _merge_kernel(h_ref, ys_ref, oa_ref, oh_ref, om_ref, g0, g1, g2, g3, ws, wa, wh, wm, wo, o_ref):
    m = (g0[...] * _dot(ys_ref[...], ws[...]) + g1[...] * _dot(oa_ref[...], wa[...])
         + g2[...] * _dot(oh_ref[...], wh[...]) + g3[...] * _dot(om_ref[...], wm[...]))
    o_ref[...] = h_ref[...] + _dot(m.astype(BF16), wo[...])


def merge_out(h, ys, oa, oh, om, z, ws, wa, wh, wm, wo, tm):
    n = h.shape[0]
    tm = min(tm, n)
    rows = lambda a: pl.BlockSpec((tm, a.shape[1]), lambda i: (i, 0))
    full = lambda a: pl.BlockSpec(a.shape, lambda i: (0, 0))
    gate = lambda k: pl.BlockSpec((tm, D_MODEL), lambda i: (i, GATE_COL0 + k))
    return pl.pallas_call(
        _merge_kernel,
        grid=(n // tm,),
        in_specs=[rows(h), rows(ys), rows(oa), rows(oh), rows(om),
                  gate(0), gate(1), gate(2), gate(3),
                  full(ws), full(wa), full(wh), full(wm), full(wo)],
        out_specs=rows(h),
        out_shape=SDS(h.shape, F32),
        compiler_params=_cparams(1),
        name="merge_out",
    )(h, ys, oa, oh, om, z, z, z, z, ws, wa, wh, wm, wo)


def _ffn_kernel(h_ref, g_ref, wg_ref, wu_ref, wo_ref, gf_ref, o_ref, hn_sc, acc_sc, *, final):
    j = pl.program_id(1)
    nj = pl.num_programs(1)

    @pl.when(j == 0)
    def _():
        hn_sc[...] = _rms(h_ref[...], g_ref[...]).astype(BF16)
        acc_sc[...] = jnp.zeros(acc_sc.shape, F32)

    hn = hn_sc[...]
    act = _silu(_dot(hn, wg_ref[...])) * _dot(hn, wu_ref[...])
    acc_sc[...] += _dot(act.astype(BF16), wo_ref[...])

    @pl.when(j == nj - 1)
    def _():
        out = h_ref[...] + acc_sc[...]
        o_ref[...] = _rms(out, gf_ref[...]) if final else out


def ffn(h, g, w_in, w_out, gf, tm, tf, final):
    n = h.shape[0]
    tm = min(tm, n)
    nf = D_FF // tf
    return pl.pallas_call(
        functools.partial(_ffn_kernel, final=final),
        grid=(n // tm, nf),
        in_specs=[pl.BlockSpec((tm, D_MODEL), lambda i, j: (i, 0)),
                  pl.BlockSpec((1, D_MODEL), lambda i, j: (0, 0)),
                  pl.BlockSpec((D_MODEL, tf), lambda i, j: (0, j)),
                  pl.BlockSpec((D_MODEL, tf), lambda i, j: (0, nf + j)),
                  pl.BlockSpec((tf, D_MODEL), lambda i, j: (j, 0)),
                  pl.BlockSpec((1, D_MODEL), lambda i, j: (0, 0))],
        out_specs=pl.BlockSpec((tm, D_MODEL), lambda i, j: (i, 0)),
        out_shape=SDS(h.shape, F32),
        scratch_shapes=[pltpu.VMEM((tm, D_MODEL), BF16), pltpu.VMEM((tm, D_MODEL), F32)],
        compiler_params=_cparams(2),
        name="ffn",
    )(h, g, w_in, w_in, w_out, gf)


def _rope_tables(pos):
    inv = ROPE_THETA ** (-jnp.arange(0, 64, 2, dtype=F32) / 64)
    ang = pos[:, None] * inv[None, :]
    ang = jnp.concatenate([ang, ang, ang, ang], axis=-1)
    cos, sin = jnp.cos(ang), jnp.sin(ang)
    upper = (jnp.arange(HEAD_LANES) % 64) >= 32
    return cos, jnp.where(upper, sin, 0.0), jnp.where(upper, 0.0, -sin)


def _block_diag(a):
    g, r, c = a.shape
    eye = jnp.eye(g, dtype=a.dtype)
    return (a[:, :, None, :] * eye[:, None, :, None]).reshape(g * r, g * c)


def _s5_params(lam_re, lam_im, log_dt, b_re, b_im, c_re, c_im, d):
    dt = jnp.exp(log_dt.astype(F32))[:, None]
    lr, li = lam_re.astype(F32), lam_im.astype(F32)
    mag = jnp.exp(lr * dt)
    ar, ai = mag * jnp.cos(li * dt), mag * jnp.sin(li * dt)
    nr, ni = ar - 1.0, ai
    den = lr * lr + li * li
    cr, ci = (nr * lr + ni * li) / den, (ni * lr - nr * li) / den
    br, bi = b_re.astype(F32), b_im.astype(F32)
    bbr = cr[..., None] * br - ci[..., None] * bi
    bbi = cr[..., None] * bi + ci[..., None] * br
    ar, ai = ar.reshape(1, -1), ai.reshape(1, -1)
    pows_r, pows_i = [ar], [ai]
    for _ in range(CHUNK - 1):
        pr, pi = pows_r[-1], pows_i[-1]
        pows_r.append(pr * ar - pi * ai)
        pows_i.append(pr * ai + pi * ar)
    rep = lambda x: jnp.broadcast_to(x, (CHUNK, x.shape[-1]))
    pw = jnp.stack([rep(pows_r[0]), rep(pows_i[0]), rep(pows_r[1]), rep(pows_i[1]),
                    rep(pows_r[3]), rep(pows_i[3]),
                    jnp.concatenate(pows_r, axis=0), jnp.concatenate(pows_i, axis=0)])
    return {
        "bre": _block_diag(jnp.swapaxes(bbr, 1, 2)).astype(BF16),
        "bim": _block_diag(jnp.swapaxes(bbi, 1, 2)).astype(BF16),
        "cre": _block_diag(jnp.swapaxes(c_re.astype(F32), 1, 2)).astype(BF16),
        "cim": _block_diag(jnp.swapaxes(c_im.astype(F32), 1, 2)).astype(BF16),
        "d": d.astype(F32).reshape(1, -1),
        "pw": pw,
    }


def _state_to_rows(s):
    b, h, dk, dv = s.shape
    return jnp.swapaxes(s, 2, 3).reshape(b, h * dv, dk)


def _rows_to_state(s, h):
    b, hv, dk = s.shape
    return jnp.swapaxes(s.reshape(b, h, hv // h, dk), 2, 3)


def kernel(x_prompt, x_sample, mem_prompt, cache_k, cache_v, page_table, cache_mem_k, cache_mem_v, state_ssm_re, state_ssm_im, state_hgrn, norm_mix, w_in, ssm_lam_re, ssm_lam_im, ssm_log_dt, ssm_b_re, ssm_b_im, ssm_c_re, ssm_c_im, ssm_d, ssm_w_glu, att_lq1, att_lk1, att_lq2, att_lk2, att_norm, hg_gamma, hg_norm, mem_norm, w_mem_kv, wb_ssm, wb_att, wb_hg, wb_mem, w_out, norm_ffn, w_ffn_in, w_ffn_out, norm_final):
    depth = w_in.shape[0]
    b, s, _ = x_prompt.shape
    db, ds, _ = x_sample.shape
    n_pages, page = page_table.shape[1], cache_k.shape[2]
    past = n_pages * page
    n_phys = cache_k.shape[1]
    g_ssm, p_ssm = state_ssm_re.shape[2], state_ssm_re.shape[3]
    hgw = hg_gamma.shape[1]
    memw = w_mem_kv.shape[2] // 2

    u_w = ssm_b_re.shape[1] * ssm_b_re.shape[3]
    split = IN_WIDTH - N_HEADS * D_MODEL
    w_in_p = jnp.concatenate([w_in[:, :, u_w:split], w_in[:, :, :u_w], w_in[:, :, split:]],
                             axis=-1).astype(BF16)
    bf = lambda a: a.astype(BF16)
    w_glu, w_mkv = bf(ssm_w_glu), bf(w_mem_kv)
    wbs, wba, wbh, wbm, wo = bf(wb_ssm), bf(wb_att), bf(wb_hg), bf(wb_mem), bf(w_out)
    wfi, wfo = bf(w_ffn_in), bf(w_ffn_out)
    gam = jax.nn.softmax(hg_gamma.astype(F32), axis=0)
    lb_all = jnp.cumsum(gam, axis=0) - gam[0]

    tabs_p = _rope_tables(jnp.arange(s, dtype=F32))
    tabs_s = tuple(jnp.tile(t, (db, 1)) for t in _rope_tables(past + jnp.arange(ds, dtype=F32)))

    ck = cache_k.reshape(depth, n_phys, page, N_HEADS * HEAD_LANES)
    cv = cache_v.reshape(depth, n_phys, page, N_HEADS * HEAD_LANES)
    cmk = cache_mem_k.reshape(depth * db * MEM_TOKENS, memw)
    cmv = cache_mem_v.reshape(depth * db * MEM_TOKENS, memw)
    memp = mem_prompt.reshape(b * MEM_TOKENS, D_MODEL)

    hp = x_prompt.reshape(b * s, D_MODEL)
    hs = x_sample.reshape(db * ds, D_MODEL)
    zeros_p = jnp.zeros((b, 1, g_ssm * p_ssm), F32)
    zeros_hg = jnp.zeros((b, hgw, hgw // N_HEADS), F32)

    tm_p = min(1024, s)
    tm_s = min(1024, db * ds)
    tt_p = min(512, s)
    tq_p = min(256, s)

    outs = {k: [] for k in ("kp", "vp", "mkp", "mvp", "srp", "sip", "hgp", "ks", "vs", "srs", "sis", "hgs")}
    for l in range(depth):
        lam_init = 0.8 - 0.6 * math.exp(-0.3 * l)
        cst = jnp.full((1,), lam_init, F32)
        lyr = jnp.full((1,), l, jnp.int32)
        row = lambda a: a[l].astype(F32).reshape(1, -1)
        s5p = _s5_params(ssm_lam_re[l], ssm_lam_im[l], ssm_log_dt[l], ssm_b_re[l], ssm_b_im[l],
                         ssm_c_re[l], ssm_c_im[l], ssm_d[l])
        lq1, lk1, lq2, lk2 = row(att_lq1), row(att_lk1), row(att_lq2), row(att_lk2)
        gn_att = row(att_norm)
        gn_hg = jnp.tile(row(hg_norm), (1, N_HEADS))
        lb = lb_all[l].reshape(1, -1)
        final = l == depth - 1
        gfin = norm_final.astype(F32).reshape(1, -1)

        mkv = mem_kv_proj(memp, row(mem_norm), w_mkv[l])
        z, zb = in_proj(hp, row(norm_mix), w_in_p[l], *tabs_p, tm_p)
        ys, sr, si = s5_mixer(z, zeros_p, zeros_p, s5p, w_glu[l], b, s, tt_p)
        oa = attn_prompt(cst, zb, lq1, lk1, lq2, lk2, gn_att, b, s, tq_p)
        oh, sh = hgrn_mixer(z, lb, gn_hg, zeros_hg, b, s, tt_p)
        om = mem_attn(jnp.zeros((1,), jnp.int32), z, mkv, mkv, 0, 1, b, s, tt_p)
        hp = merge_out(hp, ys, oa, oh, om, z, wbs[l], wba[l], wbh[l], wbm[l], wo[l], 256)
        hp = ffn(hp, row(norm_ffn), wfi[l], wfo[l], gfin, 512, D_FF // 2, final)
        outs["kp"].append(z[:, 512:1024].reshape(b, s, N_HEADS, HEAD_LANES))
        outs["vp"].append(z[:, 1024:1536].reshape(b, s, N_HEADS, HEAD_LANES))
        outs["mkp"].append(mkv[:, :memw].reshape(b, MEM_TOKENS, N_HEADS, memw // N_HEADS))
        outs["mvp"].append(mkv[:, memw:].reshape(b, MEM_TOKENS, N_HEADS, memw // N_HEADS))
        outs["srp"].append(sr.reshape(b, g_ssm, p_ssm))
        outs["sip"].append(si.reshape(b, g_ssm, p_ssm))
        outs["hgp"].append(_rows_to_state(sh, N_HEADS))

        z, zb = in_proj(hs, row(norm_mix), w_in_p[l], *tabs_s, tm_s)
        ys, sr, si = s5_mixer(z, state_ssm_re[l].reshape(db, 1, -1), state_ssm_im[l].reshape(db, 1, -1),
                              s5p, w_glu[l], db, ds, ds)
        oa = attn_sample(page_table, lyr, cst, zb, ck, cv, lq1, lk1, lq2, lk2, gn_att, ds)
        oh, sh = hgrn_mixer(z, lb, gn_hg, _state_to_rows(state_hgrn[l]), db, ds, ds)
        om = mem_attn(jnp.full((1,), l * db, jnp.int32), z, cmk, cmv, 0, 0, db, ds, ds)
        hs = merge_out(hs, ys, oa, oh, om, z, wbs[l], wba[l], wbh[l], wbm[l], wo[l], 256)
        hs = ffn(hs, row(norm_ffn), wfi[l], wfo[l], gfin, 512, D_FF // 2, final)
        outs["ks"].append(z[:, 512:1024].reshape(db, ds, N_HEADS, HEAD_LANES))
        outs["vs"].append(z[:, 1024:1536].reshape(db, ds, N_HEADS, HEAD_LANES))
        outs["srs"].append(sr.reshape(db, g_ssm, p_ssm))
        outs["sis"].append(si.reshape(db, g_ssm, p_ssm))
        outs["hgs"].append(_rows_to_state(sh, N_HEADS))

    st = jnp.stack
    return (hp.reshape(b, s, D_MODEL), hs.reshape(db, ds, D_MODEL),
            st(outs["kp"]), st(outs["vp"]), st(outs["mkp"]), st(outs["mvp"]),
            st(outs["srp"]), st(outs["sip"]), st(outs["hgp"]),
            st(outs["ks"]), st(outs["vs"]), st(outs["srs"]), st(outs["sis"]), st(outs["hgs"]))
```

```python
import functools
import math

import jax
import jax.numpy as jnp
from jax import lax
from jax.experimental import pallas as pl
from jax.experimental.pallas import tpu as pltpu

F32 = jnp.float32
BF16 = jnp.bfloat16
SDS = jax.ShapeDtypeStruct

D_MODEL = 1024
EPS = 1e-6
ROPE_THETA = 10000.0
MASK_VALUE = -1e30
HG_F_FLOOR = 1e-20
N_HEADS = 4
HEAD_LANES = 128
MEM_TOKENS = 256
D_FF = 2816
IN_WIDTH = 7168
IN_TILE = 512
CHUNK = 8
Q_SCALE = 64 ** -0.5 * math.log2(math.e)
VT_ROWS = HEAD_LANES + 16
VMEM_LIMIT = 56 * 1024 * 1024

COL_HQ, COL_HF, COL_HI, COL_HG, COL_MQ, COL_U = 6, 7, 8, 9, 10, 11
GATE_COL0 = 3


def _cparams(n_axes):
    return pltpu.CompilerParams(dimension_semantics=("arbitrary",) * n_axes,
                                vmem_limit_bytes=VMEM_LIMIT)


def _sigmoid(x):
    return 1.0 / (1.0 + jnp.exp(-x))


def _silu(x):
    return x * _sigmoid(x)


def _rms(x, g):
    return x * lax.rsqrt(jnp.mean(x * x, axis=-1, keepdims=True) + EPS) * g


def _dot(a, b):
    return jnp.dot(a, b, preferred_element_type=F32)


def _dot_nt(a, b):
    return lax.dot_general(a, b, (((1,), (1,)), ((), ())), preferred_element_type=F32)


def _block_ones(n, blk, dtype):
    r = lax.broadcasted_iota(jnp.int32, (n, n), 0) // blk
    c = lax.broadcasted_iota(jnp.int32, (n, n), 1) // blk
    return r == c


def _inproj_kernel(x_ref, g_ref, w_ref, cos_ref, sa_ref, sb_ref, z_ref, zb_ref, xn_ref):
    j = pl.program_id(1)

    @pl.when(j == 0)
    def _():
        xn_ref[...] = _rms(x_ref[...], g_ref[...]).astype(BF16)

    acc = _dot(xn_ref[...], w_ref[...])

    @pl.when(j < 2)
    def _():
        cos, sa, sb = cos_ref[...], sa_ref[...], sb_ref[...]
        scale = jnp.where(j == 0, Q_SCALE, 1.0)
        for hh in range(N_HEADS):
            sl = slice(hh * HEAD_LANES, (hh + 1) * HEAD_LANES)
            x = acc[:, sl]
            r = (x * cos + pltpu.roll(x, 32, 1) * sa + pltpu.roll(x, 96, 1) * sb)
            z_ref[:, sl] = r
            zb_ref[:, sl] = (r * scale).astype(BF16)

    @pl.when(j == 2)
    def _():
        z_ref[...] = acc
        zb_ref[...] = acc.astype(BF16)

    @pl.when(jnp.logical_and(j > 2, j < 6))
    def _():
        z_ref[...] = acc

    @pl.when(j >= 6)
    def _():
        z_ref[...] = _sigmoid(acc)


def in_proj(h, g, w, cos_t, sa_t, sb_t, tm):
    n = h.shape[0]
    nper = cos_t.shape[0] // tm
    tab = pl.BlockSpec((tm, HEAD_LANES), lambda i, j: (i % nper, 0))
    return pl.pallas_call(
        _inproj_kernel,
        grid=(n // tm, IN_WIDTH // IN_TILE),
        in_specs=[pl.BlockSpec((tm, D_MODEL), lambda i, j: (i, 0)),
                  pl.BlockSpec((1, D_MODEL), lambda i, j: (0, 0)),
                  pl.BlockSpec((D_MODEL, IN_TILE), lambda i, j: (0, j)),
                  tab, tab, tab],
        out_specs=[pl.BlockSpec((tm, IN_TILE), lambda i, j: (i, j)),
                   pl.BlockSpec((tm, IN_TILE), lambda i, j: (i, jnp.minimum(j, 2)))],
        out_shape=[SDS((n, IN_WIDTH), F32), SDS((n, 3 * IN_TILE), BF16)],
        scratch_shapes=[pltpu.VMEM((tm, D_MODEL), BF16)],
        compiler_params=_cparams(2),
        name="in_proj",
    )(h, g, w, cos_t, sa_t, sb_t)


def _memkv_kernel(x_ref, g_ref, w_ref, o_ref):
    o_ref[...] = _dot(_rms(x_ref[...], g_ref[...]).astype(BF16), w_ref[...])


def mem_kv_proj(x, g, w):
    n, wd = x.shape[0], w.shape[1]
    tm = 256
    return pl.pallas_call(
        _memkv_kernel,
        grid=(n // tm,),
        in_specs=[pl.BlockSpec((tm, D_MODEL), lambda i: (i, 0)),
                  pl.BlockSpec((1, D_MODEL), lambda i: (0, 0)),
                  pl.BlockSpec((D_MODEL, wd), lambda i: (0, 0))],
        out_specs=pl.BlockSpec((tm, wd), lambda i: (i, 0)),
        out_shape=SDS((n, wd), F32),
        compiler_params=_cparams(1),
        name="mem_kv_proj",
    )(x, g, w)


def _s5_kernel(u_ref, h0r_ref, h0i_ref, bre_ref, bim_ref, cre_ref, cim_ref, d_ref, wg_ref,
               pw_ref, ys_ref, sr_ref, si_ref, xr_sc, xi_sc, hr_sc, hi_sc):
    t = pl.program_id(1)
    tt = u_ref.shape[0]
    u = u_ref[...]
    ub = u.astype(BF16)
    xr_sc[...] = _dot(ub, bre_ref[...])
    xi_sc[...] = _dot(ub, bim_ref[...])

    @pl.when(t == 0)
    def _():
        hr_sc[...] = jnp.broadcast_to(h0r_ref[...], hr_sc.shape)
        hi_sc[...] = jnp.broadcast_to(h0i_ref[...], hi_sc.shape)

    row = lax.broadcasted_iota(jnp.int32, (CHUNK, 1), 0)
    apr, api = pw_ref[6], pw_ref[7]

    def body(c, carry):
        hr, hi = carry
        sl = pl.ds(pl.multiple_of(c * CHUNK, CHUNK), CHUNK)
        xr, xi = xr_sc[sl, :], xi_sc[sl, :]
        for k, s in enumerate((1, 2, 4)):
            ar, ai = pw_ref[2 * k], pw_ref[2 * k + 1]
            keep = row >= s
            pr = jnp.where(keep, pltpu.roll(xr, s, 0), 0.0)
            pi = jnp.where(keep, pltpu.roll(xi, s, 0), 0.0)
            xr, xi = xr + (ar * pr - ai * pi), xi + (ar * pi + ai * pr)
        xr, xi = xr + (apr * hr - api * hi), xi + (apr * hi + api * hr)
        xr_sc[sl, :] = xr
        xi_sc[sl, :] = xi
        return (jnp.broadcast_to(xr[CHUNK - 1:CHUNK, :], xr.shape),
                jnp.broadcast_to(xi[CHUNK - 1:CHUNK, :], xi.shape))

    hr, hi = lax.fori_loop(0, tt // CHUNK, body, (hr_sc[...], hi_sc[...]),
                           unroll=min(4, tt // CHUNK))
    hr_sc[...] = hr
    hi_sc[...] = hi
    sr_ref[...] = hr[0:1, :]
    si_ref[...] = hi[0:1, :]

    y = (_dot(xr_sc[...].astype(BF16), cre_ref[...]) - _dot(xi_sc[...].astype(BF16), cim_ref[...])
         + d_ref[...] * u)
    ys = jax.nn.gelu(y)
    gate = _sigmoid(_dot(ys.astype(BF16), wg_ref[...]))
    ys_ref[...] = (ys * gate).astype(BF16)


def s5_mixer(z, h0r, h0i, prm, wglu, nb, t_len, tt):
    nt = t_len // tt
    ns = h0r.shape[-1]
    width = prm["d"].shape[-1]
    full = lambda a: pl.BlockSpec(a.shape, lambda b, t: (0,) * a.ndim)
    st = pl.BlockSpec((None, 1, ns), lambda b, t: (b, 0, 0))
    return pl.pallas_call(
        _s5_kernel,
        grid=(nb, nt),
        in_specs=[pl.BlockSpec((tt, width), lambda b, t: (b * nt + t, COL_U)),
                  st, st,
                  full(prm["bre"]), full(prm["bim"]), full(prm["cre"]), full(prm["cim"]),
                  full(prm["d"]), full(wglu), full(prm["pw"])],
        out_specs=[pl.BlockSpec((tt, width), lambda b, t: (b * nt + t, 0)), st, st],
        out_shape=[SDS((nb * t_len, width), BF16), SDS((nb, 1, ns), F32), SDS((nb, 1, ns), F32)],
        scratch_shapes=[pltpu.VMEM((tt, ns), F32), pltpu.VMEM((tt, ns), F32),
                        pltpu.VMEM((CHUNK, ns), F32), pltpu.VMEM((CHUNK, ns), F32)],
        compiler_params=_cparams(2),
        name="s5",
    )(z, h0r, h0i, prm["bre"], prm["bim"], prm["cre"], prm["cim"], prm["d"], wglu, prm["pw"])


def _lambda(lq1, lk1, lq2, lk2, lam_init):
    return (jnp.exp(jnp.sum(lq1[...] * lk1[...], axis=-1, keepdims=True))
            - jnp.exp(jnp.sum(lq2[...] * lk2[...], axis=-1, keepdims=True)) + lam_init)


def _two_maps(q):
    lane = lax.broadcasted_iota(jnp.int32, (1, HEAD_LANES), 1)
    zero = jnp.zeros_like(q)
    return jnp.concatenate([jnp.where(lane < 64, q, zero), jnp.where(lane >= 64, q, zero)], axis=0)


def _diff_finish(o1, o2, lam, lam_init, gn):
    return _rms(o1 - lam * o2, gn) * (1.0 - lam_init)


def _attn_prompt_kernel(cst_ref, q_ref, k_ref, vt_ref, lq1, lk1, lq2, lk2, gn_ref, o_ref,
                        m_sc, acc_sc, s0_sc, s1_sc, b0_sc, b1_sc):
    qi = pl.program_id(2)
    tq = q_ref.shape[0]
    w = 2 * tq
    lam_init = cst_ref[0]
    qq = _two_maps(q_ref[...])
    m_sc[...] = jnp.full(m_sc.shape, MASK_VALUE, F32)
    acc_sc[...] = jnp.zeros(acc_sc.shape, F32)
    slots = ((s0_sc, b0_sc), (s1_sc, b1_sc))

    def scores(kb, slot):
        sl = pl.ds(pl.multiple_of(kb * tq, tq), tq)
        s = _dot_nt(k_ref[sl, :], qq)
        slots[slot][0][...] = s
        slots[slot][1][...] = jnp.max(s, axis=0, keepdims=True)

    def consume(kb, slot, masked):
        sl = pl.ds(pl.multiple_of(kb * tq, tq), tq)
        s = slots[slot][0][...]
        if masked:
            key = lax.broadcasted_iota(jnp.int32, (tq, w), 0)
            qry = lax.broadcasted_iota(jnp.int32, (tq, w), 1) % tq
            s = jnp.where(key <= qry, s, MASK_VALUE)
            bmax = jnp.max(s, axis=0, keepdims=True)
        else:
            bmax = slots[slot][1][...]
        m_prev = m_sc[...]
        m_new = jnp.maximum(m_prev, bmax)
        p = jnp.exp2(s - m_new)
        acc_sc[...] = jnp.exp2(m_prev - m_new) * acc_sc[...] + _dot(vt_ref[:, sl], p.astype(BF16))
        m_sc[...] = m_new

    def body(j, carry):
        scores(2 * j + 1, 1)
        consume(2 * j, 0, False)
        scores(2 * j + 2, 0)
        consume(2 * j + 1, 1, False)
        return carry

    scores(0, 0)
    lax.fori_loop(0, qi // 2, body, 0)

    @pl.when(qi % 2 == 0)
    def _():
        consume(qi, 0, True)

    @pl.when(qi % 2 == 1)
    def _():
        scores(qi, 1)
        consume(qi - 1, 0, False)
        consume(qi, 1, True)

    lam = _lambda(lq1, lk1, lq2, lk2, lam_init)
    acc = acc_sc[...]
    ot = acc[:HEAD_LANES, :] / acc[HEAD_LANES:HEAD_LANES + 1, :]
    o_ref[...] = _diff_finish(ot[:, :tq].T, ot[:, tq:].T, lam, lam_init, gn_ref[...]).astype(BF16)


def attn_prompt(cst, zb, vt, lq1, lk1, lq2, lk2, gn, nb, t_len, tq):
    nq = t_len // tq
    vec = pl.BlockSpec((1, 64), lambda b, h, i: (0, 0))
    return pl.pallas_call(
        _attn_prompt_kernel,
        grid=(nb, N_HEADS, nq),
        in_specs=[pl.BlockSpec(memory_space=pltpu.SMEM),
                  pl.BlockSpec((tq, HEAD_LANES), lambda b, h, i: (b * nq + i, h)),
                  pl.BlockSpec((t_len, HEAD_LANES), lambda b, h, i: (b, N_HEADS + h)),
                  pl.BlockSpec((VT_ROWS, t_len), lambda b, h, i: (b * N_HEADS + h, 0)),
                  vec, vec, vec, vec,
                  pl.BlockSpec((1, HEAD_LANES), lambda b, h, i: (0, 0))],
        out_specs=pl.BlockSpec((tq, HEAD_LANES), lambda b, h, i: (b * nq + i, h)),
        out_shape=SDS((nb * t_len, N_HEADS * HEAD_LANES), BF16),
        scratch_shapes=[pltpu.VMEM((1, 2 * tq), F32), pltpu.VMEM((VT_ROWS, 2 * tq), F32),
                        pltpu.VMEM((tq, 2 * tq), F32), pltpu.VMEM((tq, 2 * tq), F32),
                        pltpu.VMEM((1, 2 * tq), F32), pltpu.VMEM((1, 2 * tq), F32)],
        compiler_params=_cparams(3),
        name="attn_prompt",
    )(cst, zb, zb, vt, lq1, lk1, lq2, lk2, gn)


def _value_rows(zb, nb, t_len):
    v = zb[:, 2 * N_HEADS * HEAD_LANES:].reshape(nb, t_len, N_HEADS, HEAD_LANES)
    vt = jnp.transpose(v, (0, 2, 3, 1))
    extra = jnp.zeros((nb, N_HEADS, VT_ROWS - HEAD_LANES, t_len), BF16).at[:, :, 0, :].set(1.0)
    return jnp.concatenate([vt, extra], axis=2).reshape(nb * N_HEADS * VT_ROWS, t_len)


def _attn_sample_kernel(pt_ref, lyr_ref, cst_ref, q_ref, kn_ref, vn_ref, lq1, lk1, lq2, lk2, gn_ref,
                        *rest):
    n_pages = (len(rest) - 1) // 2
    k_refs, v_refs, o_ref = rest[:n_pages], rest[n_pages:2 * n_pages], rest[-1]
    t = q_ref.shape[0]
    lam_init = cst_ref[0]
    lam = _lambda(lq1, lk1, lq2, lk2, lam_init)
    hsl = [slice(hh * HEAD_LANES, (hh + 1) * HEAD_LANES) for hh in range(N_HEADS)]
    qh = [_two_maps(q_ref[:, sl]) for sl in hsl]
    qq = jnp.concatenate(qh, axis=0)
    rows, cols = N_HEADS * 2 * t, k_refs[0].shape[0]
    same_head = (lax.broadcasted_iota(jnp.int32, (rows, cols), 0) // (2 * t)
                 == lax.broadcasted_iota(jnp.int32, (rows, cols), 1) % N_HEADS)
    s_past = [jnp.where(same_head, _dot_nt(qq, k_refs[p][...].astype(BF16)), MASK_VALUE)
              for p in range(n_pages)]
    r = lax.broadcasted_iota(jnp.int32, (2 * t, t), 0) % t
    c = lax.broadcasted_iota(jnp.int32, (2 * t, t), 1)
    s_new = jnp.concatenate([jnp.where(c <= r, _dot_nt(qh[hh], kn_ref[:, hsl[hh]]), MASK_VALUE)
                             for hh in range(N_HEADS)], axis=0)
    m = s_past[0]
    for sp in s_past[1:]:
        m = jnp.maximum(m, sp)
    m = jnp.maximum(jnp.max(m, axis=-1, keepdims=True), jnp.max(s_new, axis=-1, keepdims=True))
    p_new = jnp.exp2(s_new - m)
    l = jnp.sum(p_new, axis=-1, keepdims=True)
    acc = jnp.concatenate([_dot(p_new[hh * 2 * t:(hh + 1) * 2 * t].astype(BF16), vn_ref[:, hsl[hh]])
                           for hh in range(N_HEADS)], axis=0)
    lsum = jnp.zeros(s_past[0].shape, F32)
    for p in range(n_pages):
        pp = jnp.exp2(s_past[p] - m)
        lsum = lsum + pp
        acc = acc + _dot(pp.astype(BF16), v_refs[p][...].astype(BF16))
    o = acc / (l + jnp.sum(lsum, axis=-1, keepdims=True))
    for hh in range(N_HEADS):
        o1 = o[hh * 2 * t:hh * 2 * t + t]
        o2 = o[hh * 2 * t + t:(hh + 1) * 2 * t]
        o_ref[:, hsl[hh]] = _diff_finish(o1, o2, lam, lam_init, gn_ref[...]).astype(BF16)


def attn_sample(page_table, lyr, cst, zb, cache_k, cache_v, lq1, lk1, lq2, lk2, gn, t_len):
    nb, n_pages = page_table.shape
    page_rows = cache_k.shape[2]
    width = N_HEADS * HEAD_LANES
    pt = page_table.reshape(-1)
    vec = pl.BlockSpec((1, 64), lambda b, pt, ly: (0, 0))

    def page_spec(p):
        return pl.BlockSpec((None, None, page_rows, HEAD_LANES),
                            lambda b, pt, ly: (ly[0], pt[b * n_pages + p], 0, 0))

    pages = [page_spec(p) for p in range(n_pages)]
    grid_spec = pltpu.PrefetchScalarGridSpec(
        num_scalar_prefetch=2,
        grid=(nb,),
        in_specs=[pl.BlockSpec(memory_space=pltpu.SMEM),
                  pl.BlockSpec((t_len, width), lambda b, pt, ly: (b, 0)),
                  pl.BlockSpec((t_len, width), lambda b, pt, ly: (b, 1)),
                  pl.BlockSpec((t_len, width), lambda b, pt, ly: (b, 2)),
                  vec, vec, vec, vec,
                  pl.BlockSpec((1, HEAD_LANES), lambda b, pt, ly: (0, 0))] + pages + pages,
        out_specs=pl.BlockSpec((t_len, width), lambda b, pt, ly: (b, 0)),
    )
    return pl.pallas_call(
        _attn_sample_kernel,
        grid_spec=grid_spec,
        out_shape=SDS((nb * t_len, width), BF16),
        compiler_params=_cparams(1),
        name="attn_sample",
    )(pt, lyr, cst, zb, zb, zb, lq1, lk1, lq2, lk2, gn,
      *([cache_k] * n_pages), *([cache_v] * n_pages))


def _hgrn_kernel(hq_ref, hf_ref, hi_ref, hg_ref, lb_ref, gn_ref, s0_ref, oh_ref, sout_ref,
                 s_sc, q_sc, k_sc, lf_sc, o_sc):
    t = pl.program_id(1)
    nt = pl.num_programs(1)
    tt, w = hq_ref.shape
    dk = w // N_HEADS
    blk = _block_ones(w, dk, F32)
    ones_bf = jnp.where(blk, 1.0, 0.0).astype(BF16)

    @pl.when(t == 0)
    def _():
        s0 = s0_ref[...]
        s_sc[...] = jnp.where(blk, jnp.concatenate([s0] * N_HEADS, axis=1), 0.0)

    lb = lb_ref[...]
    f = lb + (1.0 - lb) * _sigmoid(hf_ref[...])
    lf_sc[...] = jnp.log(jnp.maximum(f, HG_F_FLOOR))
    k_sc[...] = 1.0 - f
    q_sc[...] = _silu(hq_ref[...])
    row = lax.broadcasted_iota(jnp.int32, (CHUNK, 1), 0)

    def body(c, carry):
        sl = pl.ds(pl.multiple_of(c * CHUNK, CHUNK), CHUNK)
        q8, k8, v8 = q_sc[sl, :], k_sc[sl, :], hi_ref[sl, :]
        cum = lf_sc[sl, :]
        for s in (1, 2, 4):
            cum = cum + jnp.where(row >= s, pltpu.roll(cum, s, 0), 0.0)
        prods = []
        for s in range(CHUNK):
            keep = row >= s
            e = jnp.where(keep, jnp.exp(jnp.where(keep, cum - cum[s:s + 1, :], 0.0)), 0.0)
            prods.append(q8 * e * k8[s:s + 1, :])
        att = _dot(jnp.concatenate(prods, axis=0).astype(BF16), ones_bf)
        o = att[0:CHUNK, :] * v8[0:1, :]
        for s in range(1, CHUNK):
            o = o + att[s * CHUNK:(s + 1) * CHUNK, :] * v8[s:s + 1, :]
        s_old = s_sc[...]
        o = o + _dot_nt((q8 * jnp.exp(cum)).astype(BF16), s_old.astype(BF16))
        o_sc[sl, :] = o
        last = cum[CHUNK - 1:CHUNK, :]
        kd = (k8 * jnp.exp(last - cum)).astype(BF16).astype(F32)
        vb = v8.astype(BF16).astype(F32)
        upd = lax.dot_general(vb, kd, (((0,), (0,)), ((), ())), preferred_element_type=F32)
        s_sc[...] = jnp.exp(last) * s_old + jnp.where(blk, upd, 0.0)
        return carry

    lax.fori_loop(0, tt // CHUNK, body, 0, unroll=min(4, tt // CHUNK))

    o = o_sc[...]
    o2 = o * o
    hi = o2.astype(BF16)
    lo = (o2 - hi.astype(F32)).astype(BF16)
    ms = (_dot(hi, ones_bf) + _dot(lo, ones_bf)) * (1.0 / dk)
    on = o * lax.rsqrt(ms + EPS) * gn_ref[...]
    oh_ref[...] = (on * _silu(hg_ref[...])).astype(BF16)

    @pl.when(t == nt - 1)
    def _():
        sf = s_sc[...]
        acc = sf[:, 0:dk]
        for hh in range(1, N_HEADS):
            acc = acc + sf[:, hh * dk:(hh + 1) * dk]
        sout_ref[...] = acc


def hgrn_mixer(z, lb, gn, s0, nb, t_len, tt):
    nt = t_len // tt
    w = lb.shape[-1]
    dk = w // N_HEADS
    col = lambda cidx: pl.BlockSpec((tt, w), lambda b, t: (b * nt + t, cidx))
    row = pl.BlockSpec((1, w), lambda b, t: (0, 0))
    st = pl.BlockSpec((None, w, dk), lambda b, t: (b, 0, 0))
    return pl.pallas_call(
        _hgrn_kernel,
        grid=(nb, nt),
        in_specs=[col(COL_HQ), col(COL_HF), col(COL_HI), col(COL_HG), row, row, st],
        out_specs=[pl.BlockSpec((tt, w), lambda b, t: (b * nt + t, 0)), st],
        out_shape=[SDS((nb * t_len, w), BF16), SDS((nb, w, dk), F32)],
        scratch_shapes=[pltpu.VMEM((w, w), F32), pltpu.VMEM((tt, w), F32), pltpu.VMEM((tt, w), F32),
                        pltpu.VMEM((tt, w), F32), pltpu.VMEM((tt, w), F32)],
        compiler_params=_cparams(2),
        name="hgrn",
    )(z, z, z, z, lb, gn, s0)


def _mem_attn_kernel(off_ref, q_ref, mk_ref, mv_ref, o_ref):
    tq, w = q_ref.shape
    dh = w // N_HEADS
    q = q_ref[...].astype(BF16) * jnp.asarray(dh ** -0.5, BF16)
    mk = mk_ref[...].astype(BF16)
    mv = mv_ref[...].astype(BF16)
    head = lax.broadcasted_iota(jnp.int32, (1, w), 1) // dh
    out = jnp.zeros((tq, w), F32)
    for hh in range(N_HEADS):
        s = _dot_nt(jnp.where(head == hh, q, jnp.zeros_like(q)), mk)
        p = jnp.exp(s - jnp.max(s, axis=-1, keepdims=True))
        p = p / jnp.sum(p, axis=-1, keepdims=True)
        out = out + jnp.where(head == hh, _dot(p.astype(BF16), mv), 0.0)
    o_ref[...] = out.astype(BF16)


def mem_attn(off, z, mk, mv, k_col, v_col, nb, t_len, tq):
    nq = t_len // tq
    w = MEM_TOKENS
    grid_spec = pltpu.PrefetchScalarGridSpec(
        num_scalar_prefetch=1,
        grid=(nb, nq),
        in_specs=[pl.BlockSpec((tq, w), lambda b, i, off: (b * nq + i, COL_MQ)),
                  pl.BlockSpec((MEM_TOKENS, w), lambda b, i, off: (off[0] + b, k_col)),
                  pl.BlockSpec((MEM_TOKENS, w), lambda b, i, off: (off[0] + b, v_col))],
        out_specs=pl.BlockSpec((tq, w), lambda b, i, off: (b * nq + i, 0)),
    )
    return pl.pallas_call(
        _mem_attn_kernel,
        grid_spec=grid_spec,
        out_shape=SDS((nb * t_len, w), BF16),
        compiler_params=_cparams(2),
        name="mem_attn",
    )(off, z, mk, mv)


def _merge_kernel(h_ref, ys_ref, oa_ref, oh_ref, om_ref, g0, g1, g2, g3, ws, wa, wh, wm, wo, o_ref):
    m = (g0[...] * _dot(ys_ref[...], ws[...]) + g1[...] * _dot(oa_ref[...], wa[...])
         + g2[...] * _dot(oh_ref[...], wh[...]) + g3[...] * _dot(om_ref[...], wm[...]))
    o_ref[...] = h_ref[...] + _dot(m.astype(BF16), wo[...])


def merge_out(h, ys, oa, oh, om, z, ws, wa, wh, wm, wo, tm):
    n = h.shape[0]
    tm = min(tm, n)
    rows = lambda a: pl.BlockSpec((tm, a.shape[1]), lambda i: (i, 0))
    full = lambda a: pl.BlockSpec(a.shape, lambda i: (0, 0))
    gate = lambda k: pl.BlockSpec((tm, D_MODEL), lambda i: (i, GATE_COL0 + k))
    return pl.pallas_call(
        _merge_kernel,
        grid=(n // tm,),
        in_specs=[rows(h), rows(ys), rows(oa), rows(oh), rows(om),
                  gate(0), gate(1), gate(2), gate(3),
                  full(ws), full(wa), full(wh), full(wm), full(wo)],
        out_specs=rows(h),
        out_shape=SDS(h.shape, F32),
        compiler_params=_cparams(1),
        name="merge_out",
    )(h, ys, oa, oh, om, z, z, z, z, ws, wa, wh, wm, wo)


def _ffn_kernel(h_ref, g_ref, wg_ref, wu_ref, wo_ref, gf_ref, o_ref, hn_sc, acc_sc, *, final):
    j = pl.program_id(1)
    nj = pl.num_programs(1)

    @pl.when(j == 0)
    def _():
        hn_sc[...] = _rms(h_ref[...], g_ref[...]).astype(BF16)
        acc_sc[...] = jnp.zeros(acc_sc.shape, F32)

    hn = hn_sc[...]
    act = _silu(_dot(hn, wg_ref[...])) * _dot(hn, wu_ref[...])
    acc_sc[...] += _dot(act.astype(BF16), wo_ref[...])

    @pl.when(j == nj - 1)
    def _():
        out = h_ref[...] + acc_sc[...]
        o_ref[...] = _rms(out, gf_ref[...]) if final else out


def ffn(h, g, w_in, w_out, gf, tm, tf, final):
    n = h.shape[0]
    tm = min(tm, n)
    nf = D_FF // tf
    return pl.pallas_call(
        functools.partial(_ffn_kernel, final=final),
        grid=(n // tm, nf),
        in_specs=[pl.BlockSpec((tm, D_MODEL), lambda i, j: (i, 0)),
                  pl.BlockSpec((1, D_MODEL), lambda i, j: (0, 0)),
                  pl.BlockSpec((D_MODEL, tf), lambda i, j: (0, j)),
                  pl.BlockSpec((D_MODEL, tf), lambda i, j: (0, nf + j)),
                  pl.BlockSpec((tf, D_MODEL), lambda i, j: (j, 0)),
                  pl.BlockSpec((1, D_MODEL), lambda i, j: (0, 0))],
        out_specs=pl.BlockSpec((tm, D_MODEL), lambda i, j: (i, 0)),
        out_shape=SDS(h.shape, F32),
        scratch_shapes=[pltpu.VMEM((tm, D_MODEL), BF16), pltpu.VMEM((tm, D_MODEL), F32)],
        compiler_params=_cparams(2),
        name="ffn",
    )(h, g, w_in, w_in, w_out, gf)


def _rope_tables(pos):
    inv = ROPE_THETA ** (-jnp.arange(0, 64, 2, dtype=F32) / 64)
    ang = pos[:, None] * inv[None, :]
    ang = jnp.concatenate([ang, ang, ang, ang], axis=-1)
    cos, sin = jnp.cos(ang), jnp.sin(ang)
    upper = (jnp.arange(HEAD_LANES) % 64) >= 32
    return cos, jnp.where(upper, sin, 0.0), jnp.where(upper, 0.0, -sin)


def _block_diag(a):
    g, r, c = a.shape
    eye = jnp.eye(g, dtype=a.dtype)
    return (a[:, :, None, :] * eye[:, None, :, None]).reshape(g * r, g * c)


def _s5_params(lam_re, lam_im, log_dt, b_re, b_im, c_re, c_im, d):
    dt = jnp.exp(log_dt.astype(F32))[:, None]
    lr, li = lam_re.astype(F32), lam_im.astype(F32)
    mag = jnp.exp(lr * dt)
    ar, ai = mag * jnp.cos(li * dt), mag * jnp.sin(li * dt)
    nr, ni = ar - 1.0, ai
    den = lr * lr + li * li
    cr, ci = (nr * lr + ni * li) / den, (ni * lr - nr * li) / den
    br, bi = b_re.astype(F32), b_im.astype(F32)
    bbr = cr[..., None] * br - ci[..., None] * bi
    bbi = cr[..., None] * bi + ci[..., None] * br
    ar, ai = ar.reshape(1, -1), ai.reshape(1, -1)
    pows_r, pows_i = [ar], [ai]
    for _ in range(CHUNK - 1):
        pr, pi = pows_r[-1], pows_i[-1]
        pows_r.append(pr * ar - pi * ai)
        pows_i.append(pr * ai + pi * ar)
    rep = lambda x: jnp.broadcast_to(x, (CHUNK, x.shape[-1]))
    pw = jnp.stack([rep(pows_r[0]), rep(pows_i[0]), rep(pows_r[1]), rep(pows_i[1]),
                    rep(pows_r[3]), rep(pows_i[3]),
                    jnp.concatenate(pows_r, axis=0), jnp.concatenate(pows_i, axis=0)])
    return {
        "bre": _block_diag(jnp.swapaxes(bbr, 1, 2)).astype(BF16),
        "bim": _block_diag(jnp.swapaxes(bbi, 1, 2)).astype(BF16),
        "cre": _block_diag(jnp.swapaxes(c_re.astype(F32), 1, 2)).astype(BF16),
        "cim": _block_diag(jnp.swapaxes(c_im.astype(F32), 1, 2)).astype(BF16),
        "d": d.astype(F32).reshape(1, -1),
        "pw": pw,
    }


def _state_to_rows(s):
    b, h, dk, dv = s.shape
    return jnp.swapaxes(s, 2, 3).reshape(b, h * dv, dk)


def _rows_to_state(s, h):
    b, hv, dk = s.shape
    return jnp.swapaxes(s.reshape(b, h, hv // h, dk), 2, 3)


def kernel(x_prompt, x_sample, mem_prompt, cache_k, cache_v, page_table, cache_mem_k, cache_mem_v, state_ssm_re, state_ssm_im, state_hgrn, norm_mix, w_in, ssm_lam_re, ssm_lam_im, ssm_log_dt, ssm_b_re, ssm_b_im, ssm_c_re, ssm_c_im, ssm_d, ssm_w_glu, att_lq1, att_lk1, att_lq2, att_lk2, att_norm, hg_gamma, hg_norm, mem_norm, w_mem_kv, wb_ssm, wb_att, wb_hg, wb_mem, w_out, norm_ffn, w_ffn_in, w_ffn_out, norm_final):
    depth = w_in.shape[0]
    b, s, _ = x_prompt.shape
    db, ds, _ = x_sample.shape
    n_pages, page = page_table.shape[1], cache_k.shape[2]
    past = n_pages * page
    n_phys = cache_k.shape[1]
    g_ssm, p_ssm = state_ssm_re.shape[2], state_ssm_re.shape[3]
    hgw = hg_gamma.shape[1]
    memw = w_mem_kv.shape[2] // 2

    u_w = ssm_b_re.shape[1] * ssm_b_re.shape[3]
    split = IN_WIDTH - N_HEADS * D_MODEL
    w_in_p = jnp.concatenate([w_in[:, :, u_w:split], w_in[:, :, :u_w], w_in[:, :, split:]],
                             axis=-1).astype(BF16)
    bf = lambda a: a.astype(BF16)
    w_glu, w_mkv = bf(ssm_w_glu), bf(w_mem_kv)
    wbs, wba, wbh, wbm, wo = bf(wb_ssm), bf(wb_att), bf(wb_hg), bf(wb_mem), bf(w_out)
    wfi, wfo = bf(w_ffn_in), bf(w_ffn_out)
    gam = jax.nn.softmax(hg_gamma.astype(F32), axis=0)
    lb_all = jnp.cumsum(gam, axis=0) - gam[0]

    tabs_p = _rope_tables(jnp.arange(s, dtype=F32))
    tabs_s = tuple(jnp.tile(t, (db, 1)) for t in _rope_tables(past + jnp.arange(ds, dtype=F32)))

    ck = cache_k.reshape(depth, n_phys, page * N_HEADS, HEAD_LANES)
    cv = cache_v.reshape(depth, n_phys, page * N_HEADS, HEAD_LANES)
    cmk = cache_mem_k.reshape(depth * db * MEM_TOKENS, memw)
    cmv = cache_mem_v.reshape(depth * db * MEM_TOKENS, memw)
    memp = mem_prompt.reshape(b * MEM_TOKENS, D_MODEL)

    hp = x_prompt.reshape(b * s, D_MODEL)
    hs = x_sample.reshape(db * ds, D_MODEL)
    zeros_p = jnp.zeros((b, 1, g_ssm * p_ssm), F32)
    zeros_hg = jnp.zeros((b, hgw, hgw // N_HEADS), F32)

    tm_p = min(1024, s)
    tm_s = min(1024, db * ds)
    tt_p = min(512, s)
    tq_p = min(512, s)

    outs = {k: [] for k in ("kp", "vp", "mkp", "mvp", "srp", "sip", "hgp", "ks", "vs", "srs", "sis", "hgs")}
    for l in range(depth):
        lam_init = 0.8 - 0.6 * math.exp(-0.3 * l)
        cst = jnp.full((1,), lam_init, F32)
        lyr = jnp.full((1,), l, jnp.int32)
        row = lambda a: a[l].astype(F32).reshape(1, -1)
        s5p = _s5_params(ssm_lam_re[l], ssm_lam_im[l], ssm_log_dt[l], ssm_b_re[l], ssm_b_im[l],
                         ssm_c_re[l], ssm_c_im[l], ssm_d[l])
        lq1, lk1, lq2, lk2 = row(att_lq1), row(att_lk1), row(att_lq2), row(att_lk2)
        gn_att = row(att_norm)
        gn_hg = jnp.tile(row(hg_norm), (1, N_HEADS))
        lb = lb_all[l].reshape(1, -1)
        final = l == depth - 1
        gfin = norm_final.astype(F32).reshape(1, -1)

        mkv = mem_kv_proj(memp, row(mem_norm), w_mkv[l])
        z, zb = in_proj(hp, row(norm_mix), w_in_p[l], *tabs_p, tm_p)
        ys, sr, si = s5_mixer(z, zeros_p, zeros_p, s5p, w_glu[l], b, s, tt_p)
        oa = attn_prompt(cst, zb, _value_rows(zb, b, s), lq1, lk1, lq2, lk2, gn_att, b, s, tq_p)
        oh, sh = hgrn_mixer(z, lb, gn_hg, zeros_hg, b, s, tt_p)
        om = mem_attn(jnp.zeros((1,), jnp.int32), z, mkv, mkv, 0, 1, b, s, tt_p)
        hp = merge_out(hp, ys, oa, oh, om, z, wbs[l], wba[l], wbh[l], wbm[l], wo[l], 256)
        hp = ffn(hp, row(norm_ffn), wfi[l], wfo[l], gfin, 512, D_FF // 2, final)
        outs["kp"].append(z[:, 512:1024].reshape(b, s, N_HEADS, HEAD_LANES))
        outs["vp"].append(z[:, 1024:1536].reshape(b, s, N_HEADS, HEAD_LANES))
        outs["mkp"].append(mkv[:, :memw].reshape(b, MEM_TOKENS, N_HEADS, memw // N_HEADS))
        outs["mvp"].append(mkv[:, memw:].reshape(b, MEM_TOKENS, N_HEADS, memw // N_HEADS))
        outs["srp"].append(sr.reshape(b, g_ssm, p_ssm))
        outs["sip"].append(si.reshape(b, g_ssm, p_ssm))
        outs["hgp"].append(_rows_to_state(sh, N_HEADS))

        z, zb = in_proj(hs, row(norm_mix), w_in_p[l], *tabs_s, tm_s)
        ys, sr, si = s5_mixer(z, state_ssm_re[l].reshape(db, 1, -1), state_ssm_im[l].reshape(db, 1, -1),
                              s5p, w_glu[l], db, ds, ds)
        oa = attn_sample(page_table, lyr, cst, zb, ck, cv, lq1, lk1, lq2, lk2, gn_att, ds)
        oh, sh = hgrn_mixer(z, lb, gn_hg, _state_to_rows(state_hgrn[l]), db, ds, ds)
        om = mem_attn(jnp.full((1,), l * db, jnp.int32), z, cmk, cmv, 0, 0, db, ds, ds)
        hs = merge_out(hs, ys, oa, oh, om, z, wbs[l], wba[l], wbh[l], wbm[l], wo[l], 256)
        hs = ffn(hs, row(norm_ffn), wfi[l], wfo[l], gfin, 512, D_FF // 2, final)
        outs["ks"].append(z[:, 512:1024].reshape(db, ds, N_HEADS, HEAD_LANES))
        outs["vs"].append(z[:, 1024:1536].reshape(db, ds, N_HEADS, HEAD_LANES))
        outs["srs"].append(sr.reshape(db, g_ssm, p_ssm))
        outs["sis"].append(si.reshape(db, g_ssm, p_ssm))
        outs["hgs"].append(_rows_to_state(sh, N_HEADS))

    st = jnp.stack
    return (hp.reshape(b, s, D_MODEL), hs.reshape(db, ds, D_MODEL),
            st(outs["kp"]), st(outs["vp"]), st(outs["mkp"]), st(outs["mvp"]),
            st(outs["srp"]), st(outs["sip"]), st(outs["hgp"]),
            st(outs["ks"]), st(outs["vs"]), st(outs["srs"]), st(outs["sis"]), st(outs["hgs"]))
```

```python
import functools
import math

import jax
import jax.numpy as jnp
from jax import lax
from jax.experimental import pallas as pl
from jax.experimental.pallas import tpu as pltpu

F32 = jnp.float32
BF16 = jnp.bfloat16
SDS = jax.ShapeDtypeStruct

D_MODEL = 1024
EPS = 1e-6
ROPE_THETA = 10000.0
MASK_VALUE = -1e30
HG_F_FLOOR = 1e-20
N_HEADS = 4
HEAD_LANES = 128
MEM_TOKENS = 256
D_FF = 2816
IN_WIDTH = 7168
IN_TILE = 512
CHUNK = 8
Q_SCALE = 64 ** -0.5 * math.log2(math.e)
VT_ROWS = HEAD_LANES + 16
VMEM_LIMIT = 56 * 1024 * 1024

COL_HQ, COL_HF, COL_HI, COL_HG, COL_MQ, COL_U = 6, 7, 8, 9, 10, 11
GATE_COL0 = 3


def _cparams(n_axes):
    return pltpu.CompilerParams(dimension_semantics=("arbitrary",) * n_axes,
                                vmem_limit_bytes=VMEM_LIMIT)


def _sigmoid(x):
    return 0.5 * jnp.tanh(0.5 * x) + 0.5


def _silu(x):
    return x * _sigmoid(x)


def _rms(x, g):
    return x * lax.rsqrt(jnp.mean(x * x, axis=-1, keepdims=True) + EPS) * g


def _dot(a, b):
    return jnp.dot(a, b, preferred_element_type=F32)


def _dot_nt(a, b):
    return lax.dot_general(a, b, (((1,), (1,)), ((), ())), preferred_element_type=F32)


def _block_ones(n, blk, dtype):
    r = lax.broadcasted_iota(jnp.int32, (n, n), 0) // blk
    c = lax.broadcasted_iota(jnp.int32, (n, n), 1) // blk
    return r == c


def _inproj_kernel(x_ref, g_ref, w_ref, cos_ref, sa_ref, sb_ref, z_ref, zb_ref, kr_ref, vr_ref, xn_ref):
    j = pl.program_id(1)
    tm = x_ref.shape[0]
    half = IN_TILE // 2

    @pl.when(j == 0)
    def _():
        xn_ref[...] = _rms(x_ref[...], g_ref[...]).astype(BF16)

    def halves():
        for c in range(2):
            yield c, slice(c * half, (c + 1) * half), _dot(xn_ref[...], w_ref[:, c * half:(c + 1) * half])

    def rotary(scale, rows_ref):
        cos, sa, sb = cos_ref[...], sa_ref[...], sb_ref[...]
        for c, _, acc in halves():
            for hh in range(2 * c, 2 * c + 2):
                sl = slice(hh * HEAD_LANES, (hh + 1) * HEAD_LANES)
                x = acc[:, (hh - 2 * c) * HEAD_LANES:(hh - 2 * c + 1) * HEAD_LANES]
                r = (x * cos + pltpu.roll(x, 32, 1) * sa + pltpu.roll(x, 96, 1) * sb)
                z_ref[:, sl] = r
                zb_ref[:, sl] = (r * scale).astype(BF16) if scale is not None else r.astype(BF16)
                if rows_ref is not None:
                    rows_ref[pl.ds(hh, tm, stride=N_HEADS), :] = r

    @pl.when(j == 0)
    def _():
        rotary(Q_SCALE, None)

    @pl.when(j == 1)
    def _():
        rotary(None, kr_ref)

    @pl.when(j == 2)
    def _():
        for c, sl, acc in halves():
            z_ref[:, sl] = acc
            zb_ref[:, sl] = acc.astype(BF16)
            for hh in range(2 * c, 2 * c + 2):
                vr_ref[pl.ds(hh, tm, stride=N_HEADS), :] = (
                    acc[:, (hh - 2 * c) * HEAD_LANES:(hh - 2 * c + 1) * HEAD_LANES])

    @pl.when(jnp.logical_and(j > 2, j < 6))
    def _():
        for _, sl, acc in halves():
            z_ref[:, sl] = acc

    @pl.when(j >= 6)
    def _():
        for _, sl, acc in halves():
            z_ref[:, sl] = _sigmoid(acc)


def in_proj(h, g, w, cos_t, sa_t, sb_t, tm):
    n = h.shape[0]
    nper = cos_t.shape[0] // tm
    tab = pl.BlockSpec((tm, HEAD_LANES), lambda i, j: (i % nper, 0))
    rows = pl.BlockSpec((tm * N_HEADS, HEAD_LANES), lambda i, j: (i, 0))
    return pl.pallas_call(
        _inproj_kernel,
        grid=(n // tm, IN_WIDTH // IN_TILE),
        in_specs=[pl.BlockSpec((tm, D_MODEL), lambda i, j: (i, 0)),
                  pl.BlockSpec((1, D_MODEL), lambda i, j: (0, 0)),
                  pl.BlockSpec((D_MODEL, IN_TILE), lambda i, j: (0, j)),
                  tab, tab, tab],
        out_specs=[pl.BlockSpec((tm, IN_TILE), lambda i, j: (i, j)),
                   pl.BlockSpec((tm, IN_TILE), lambda i, j: (i, jnp.minimum(j, 2))),
                   rows, rows],
        out_shape=[SDS((n, IN_WIDTH), F32), SDS((n, 3 * IN_TILE), BF16),
                   SDS((n * N_HEADS, HEAD_LANES), F32), SDS((n * N_HEADS, HEAD_LANES), F32)],
        scratch_shapes=[pltpu.VMEM((tm, D_MODEL), BF16)],
        compiler_params=_cparams(2),
        name="in_proj",
    )(h, g, w, cos_t, sa_t, sb_t)


def _memkv_kernel(x_ref, g_ref, w_ref, o_ref):
    o_ref[...] = _dot(_rms(x_ref[...], g_ref[...]).astype(BF16), w_ref[...])


def mem_kv_proj(x, g, w):
    n, wd = x.shape[0], w.shape[1]
    tm = 256
    return pl.pallas_call(
        _memkv_kernel,
        grid=(n // tm,),
        in_specs=[pl.BlockSpec((tm, D_MODEL), lambda i: (i, 0)),
                  pl.BlockSpec((1, D_MODEL), lambda i: (0, 0)),
                  pl.BlockSpec((D_MODEL, wd), lambda i: (0, 0))],
        out_specs=pl.BlockSpec((tm, wd), lambda i: (i, 0)),
        out_shape=SDS((n, wd), F32),
        compiler_params=_cparams(1),
        name="mem_kv_proj",
    )(x, g, w)


def _s5_kernel(u_ref, h0r_ref, h0i_ref, bre_ref, bim_ref, cre_ref, cim_ref, d_ref, wg_ref,
               pw_ref, ys_ref, sr_ref, si_ref, xr_sc, xi_sc, hr_sc, hi_sc, *, per_seq):
    t = pl.program_id(1)
    tt = u_ref.shape[0]
    ns = xr_sc.shape[1]
    n_chunks = tt // CHUNK
    u = u_ref[...]
    ub = u.astype(BF16)
    xr_sc[...] = _dot(ub, bre_ref[...])
    xi_sc[...] = _dot(ub, bim_ref[...])
    row = lax.broadcasted_iota(jnp.int32, (CHUNK, 1), 0)
    apr, api = pw_ref[6], pw_ref[7]

    def scan_chunk(c, hr, hi):
        sl = pl.ds(pl.multiple_of(c * CHUNK, CHUNK), CHUNK)
        xr, xi = xr_sc[sl, :], xi_sc[sl, :]
        for k, s in enumerate((1, 2, 4)):
            ar, ai = pw_ref[2 * k], pw_ref[2 * k + 1]
            keep = row >= s
            pr = jnp.where(keep, pltpu.roll(xr, s, 0), 0.0)
            pi = jnp.where(keep, pltpu.roll(xi, s, 0), 0.0)
            xr, xi = xr + (ar * pr - ai * pi), xi + (ar * pi + ai * pr)
        xr, xi = xr + (apr * hr - api * hi), xi + (apr * hi + api * hr)
        xr_sc[sl, :] = xr
        xi_sc[sl, :] = xi
        return xr, xi

    if per_seq:
        def body(c, carry):
            one = pl.ds(c, 1)
            xr, xi = scan_chunk(c, jnp.broadcast_to(h0r_ref[one, :], (CHUNK, ns)),
                                jnp.broadcast_to(h0i_ref[one, :], (CHUNK, ns)))
            sr_ref[one, :] = xr[CHUNK - 1:CHUNK, :]
            si_ref[one, :] = xi[CHUNK - 1:CHUNK, :]
            return carry

        lax.fori_loop(0, n_chunks, body, 0, unroll=math.gcd(4, n_chunks))
    else:
        @pl.when(t == 0)
        def _():
            hr_sc[...] = jnp.broadcast_to(h0r_ref[...], hr_sc.shape)
            hi_sc[...] = jnp.broadcast_to(h0i_ref[...], hi_sc.shape)

        def body(c, carry):
            xr, xi = scan_chunk(c, *carry)
            return (jnp.broadcast_to(xr[CHUNK - 1:CHUNK, :], xr.shape),
                    jnp.broadcast_to(xi[CHUNK - 1:CHUNK, :], xi.shape))

        hr, hi = lax.fori_loop(0, n_chunks, body, (hr_sc[...], hi_sc[...]),
                               unroll=math.gcd(4, n_chunks))
        hr_sc[...] = hr
        hi_sc[...] = hi
        sr_ref[...] = hr[0:1, :]
        si_ref[...] = hi[0:1, :]

    y = (_dot(xr_sc[...].astype(BF16), cre_ref[...]) - _dot(xi_sc[...].astype(BF16), cim_ref[...])
         + d_ref[...] * u)
    ys = jax.nn.gelu(y)
    gate = _sigmoid(_dot(ys.astype(BF16), wg_ref[...]))
    ys_ref[...] = (ys * gate).astype(BF16)


def s5_mixer(z, h0r, h0i, prm, wglu, nb, t_len, tt):
    per_seq = t_len == CHUNK
    ns = h0r.shape[-1]
    width = prm["d"].shape[-1]
    full = lambda a: pl.BlockSpec(a.shape, lambda b, t: (0,) * a.ndim)
    if per_seq:
        grid, tt = (1, 1), nb * t_len
        h0r, h0i = h0r.reshape(nb, ns), h0i.reshape(nb, ns)
        st = pl.BlockSpec((nb, ns), lambda b, t: (0, 0))
        rows = lambda cidx: pl.BlockSpec((tt, width), lambda b, t: (0, cidx))
    else:
        nt = t_len // tt
        grid = (nb, nt)
        st = pl.BlockSpec((None, 1, ns), lambda b, t: (b, 0, 0))
        rows = lambda cidx: pl.BlockSpec((tt, width), lambda b, t: (b * nt + t, cidx))
    return pl.pallas_call(
        functools.partial(_s5_kernel, per_seq=per_seq),
        grid=grid,
        in_specs=[rows(COL_U), st, st,
                  full(prm["bre"]), full(prm["bim"]), full(prm["cre"]), full(prm["cim"]),
                  full(prm["d"]), full(wglu), full(prm["pw"])],
        out_specs=[rows(0), st, st],
        out_shape=[SDS((nb * t_len, width), BF16), SDS(h0r.shape, F32), SDS(h0r.shape, F32)],
        scratch_shapes=[pltpu.VMEM((tt, ns), F32), pltpu.VMEM((tt, ns), F32),
                        pltpu.VMEM((CHUNK, ns), F32), pltpu.VMEM((CHUNK, ns), F32)],
        compiler_params=_cparams(2),
        name="s5",
    )(z, h0r, h0i, prm["bre"], prm["bim"], prm["cre"], prm["cim"], prm["d"], wglu, prm["pw"])


def _lambda(lq1, lk1, lq2, lk2, lam_init):
    return (jnp.exp(jnp.sum(lq1[...] * lk1[...], axis=-1, keepdims=True))
            - jnp.exp(jnp.sum(lq2[...] * lk2[...], axis=-1, keepdims=True)) + lam_init)


def _two_maps(q):
    lane = lax.broadcasted_iota(jnp.int32, (1, HEAD_LANES), 1)
    zero = jnp.zeros_like(q)
    return jnp.concatenate([jnp.where(lane < 64, q, zero), jnp.where(lane >= 64, q, zero)], axis=0)


def _diff_finish(o1, o2, lam, lam_init, gn):
    return _rms(o1 - lam * o2, gn) * (1.0 - lam_init)


def _attn_prompt_kernel(cst_ref, q_ref, k_ref, vt_ref, lq1, lk1, lq2, lk2, gn_ref, o_ref,
                        m_sc, acc_sc, s0_sc, s1_sc, b0_sc, b1_sc):
    qi = pl.program_id(2)
    tq = q_ref.shape[0]
    w = 2 * tq
    lam_init = cst_ref[0]
    qq = _two_maps(q_ref[...])
    m_sc[...] = jnp.full(m_sc.shape, MASK_VALUE, F32)
    acc_sc[...] = jnp.zeros(acc_sc.shape, F32)
    slots = ((s0_sc, b0_sc), (s1_sc, b1_sc))

    def scores(kb, slot):
        sl = pl.ds(pl.multiple_of(kb * tq, tq), tq)
        s = _dot_nt(k_ref[sl, :], qq)
        slots[slot][0][...] = s
        slots[slot][1][...] = jnp.max(s, axis=0, keepdims=True)

    def consume(kb, slot, masked):
        sl = pl.ds(pl.multiple_of(kb * tq, tq), tq)
        s = slots[slot][0][...]
        if masked:
            key = lax.broadcasted_iota(jnp.int32, (tq, w), 0)
            qry = lax.broadcasted_iota(jnp.int32, (tq, w), 1) % tq
            s = jnp.where(key <= qry, s, MASK_VALUE)
            bmax = jnp.max(s, axis=0, keepdims=True)
        else:
            bmax = slots[slot][1][...]
        m_prev = m_sc[...]
        m_new = jnp.maximum(m_prev, bmax)
        p = jnp.exp2(s - m_new)
        acc_sc[...] = jnp.exp2(m_prev - m_new) * acc_sc[...] + _dot(vt_ref[:, sl], p.astype(BF16))
        m_sc[...] = m_new

    def body(j, carry):
        scores(2 * j + 1, 1)
        consume(2 * j, 0, False)
        scores(2 * j + 2, 0)
        consume(2 * j + 1, 1, False)
        return carry

    scores(0, 0)
    lax.fori_loop(0, qi // 2, body, 0)

    @pl.when(qi % 2 == 0)
    def _():
        consume(qi, 0, True)

    @pl.when(qi % 2 == 1)
    def _():
        scores(qi, 1)
        consume(qi - 1, 0, False)
        consume(qi, 1, True)

    lam = _lambda(lq1, lk1, lq2, lk2, lam_init)
    acc = acc_sc[...]
    ot = acc[:HEAD_LANES, :] / acc[HEAD_LANES:HEAD_LANES + 1, :]
    o_ref[...] = _diff_finish(ot[:, :tq].T, ot[:, tq:].T, lam, lam_init, gn_ref[...]).astype(BF16)


def attn_prompt(cst, zb, vt, lq1, lk1, lq2, lk2, gn, nb, t_len, tq):
    nq = t_len // tq
    vec = pl.BlockSpec((1, 64), lambda b, h, i: (0, 0))
    return pl.pallas_call(
        _attn_prompt_kernel,
        grid=(nb, N_HEADS, nq),
        in_specs=[pl.BlockSpec(memory_space=pltpu.SMEM),
                  pl.BlockSpec((tq, HEAD_LANES), lambda b, h, i: (b * nq + i, h)),
                  pl.BlockSpec((t_len, HEAD_LANES), lambda b, h, i: (b, N_HEADS + h)),
                  pl.BlockSpec((VT_ROWS, t_len), lambda b, h, i: (b * N_HEADS + h, 0)),
                  vec, vec, vec, vec,
                  pl.BlockSpec((1, HEAD_LANES), lambda b, h, i: (0, 0))],
        out_specs=pl.BlockSpec((tq, HEAD_LANES), lambda b, h, i: (b * nq + i, h)),
        out_shape=SDS((nb * t_len, N_HEADS * HEAD_LANES), BF16),
        scratch_shapes=[pltpu.VMEM((1, 2 * tq), F32), pltpu.VMEM((VT_ROWS, 2 * tq), F32),
                        pltpu.VMEM((tq, 2 * tq), F32), pltpu.VMEM((tq, 2 * tq), F32),
                        pltpu.VMEM((1, 2 * tq), F32), pltpu.VMEM((1, 2 * tq), F32)],
        compiler_params=_cparams(3),
        name="attn_prompt",
    )(cst, zb, zb, vt, lq1, lk1, lq2, lk2, gn)


def _value_rows(zb, nb, t_len):
    v = zb[:, 2 * N_HEADS * HEAD_LANES:].reshape(nb, t_len, N_HEADS, HEAD_LANES)
    vt = jnp.transpose(v, (0, 2, 3, 1))
    extra = jnp.zeros((nb, N_HEADS, VT_ROWS - HEAD_LANES, t_len), BF16).at[:, :, 0, :].set(1.0)
    return jnp.concatenate([vt, extra], axis=2).reshape(nb * N_HEADS * VT_ROWS, t_len)


def _attn_sample_kernel(pt_ref, lyr_ref, cst_ref, q_ref, kn_ref, vn_ref, lq1, lk1, lq2, lk2, gn_ref,
                        *rest):
    n_pages = (len(rest) - 1) // 2
    k_refs, v_refs, o_ref = rest[:n_pages], rest[n_pages:2 * n_pages], rest[-1]
    t = q_ref.shape[0]
    lam_init = cst_ref[0]
    lam = _lambda(lq1, lk1, lq2, lk2, lam_init)
    hsl = [slice(hh * HEAD_LANES, (hh + 1) * HEAD_LANES) for hh in range(N_HEADS)]
    qh = [_two_maps(q_ref[:, sl]) for sl in hsl]
    qq = jnp.concatenate(qh, axis=0)
    rows, cols = N_HEADS * 2 * t, k_refs[0].shape[0]
    same_head = (lax.broadcasted_iota(jnp.int32, (rows, cols), 0) // (2 * t)
                 == lax.broadcasted_iota(jnp.int32, (rows, cols), 1) % N_HEADS)
    s_past = [jnp.where(same_head, _dot_nt(qq, k_refs[p][...].astype(BF16)), MASK_VALUE)
              for p in range(n_pages)]
    r = lax.broadcasted_iota(jnp.int32, (2 * t, t), 0) % t
    c = lax.broadcasted_iota(jnp.int32, (2 * t, t), 1)
    s_new = jnp.concatenate([jnp.where(c <= r, _dot_nt(qh[hh], kn_ref[:, hsl[hh]]), MASK_VALUE)
                             for hh in range(N_HEADS)], axis=0)
    m = s_past[0]
    for sp in s_past[1:]:
        m = jnp.maximum(m, sp)
    m = jnp.maximum(jnp.max(m, axis=-1, keepdims=True), jnp.max(s_new, axis=-1, keepdims=True))
    p_new = jnp.exp2(s_new - m)
    l = jnp.sum(p_new, axis=-1, keepdims=True)
    acc = jnp.concatenate([_dot(p_new[hh * 2 * t:(hh + 1) * 2 * t].astype(BF16), vn_ref[:, hsl[hh]])
                           for hh in range(N_HEADS)], axis=0)
    lsum = jnp.zeros(s_past[0].shape, F32)
    for p in range(n_pages):
        pp = jnp.exp2(s_past[p] - m)
        lsum = lsum + pp
        acc = acc + _dot(pp.astype(BF16), v_refs[p][...].astype(BF16))
    o = acc / (l + jnp.sum(lsum, axis=-1, keepdims=True))
    for hh in range(N_HEADS):
        o1 = o[hh * 2 * t:hh * 2 * t + t]
        o2 = o[hh * 2 * t + t:(hh + 1) * 2 * t]
        o_ref[:, hsl[hh]] = _diff_finish(o1, o2, lam, lam_init, gn_ref[...]).astype(BF16)


def attn_sample(page_table, lyr, cst, zb, cache_k, cache_v, lq1, lk1, lq2, lk2, gn, t_len):
    nb, n_pages = page_table.shape
    page_rows = cache_k.shape[2]
    width = N_HEADS * HEAD_LANES
    pt = page_table.reshape(-1)
    vec = pl.BlockSpec((1, 64), lambda b, pt, ly: (0, 0))

    def page_spec(p):
        return pl.BlockSpec((None, None, page_rows, HEAD_LANES),
                            lambda b, pt, ly: (ly[0], pt[b * n_pages + p], 0, 0))

    pages = [page_spec(p) for p in range(n_pages)]
    grid_spec = pltpu.PrefetchScalarGridSpec(
        num_scalar_prefetch=2,
        grid=(nb,),
        in_specs=[pl.BlockSpec(memory_space=pltpu.SMEM),
                  pl.BlockSpec((t_len, width), lambda b, pt, ly: (b, 0)),
                  pl.BlockSpec((t_len, width), lambda b, pt, ly: (b, 1)),
                  pl.BlockSpec((t_len, width), lambda b, pt, ly: (b, 2)),
                  vec, vec, vec, vec,
                  pl.BlockSpec((1, HEAD_LANES), lambda b, pt, ly: (0, 0))] + pages + pages,
        out_specs=pl.BlockSpec((t_len, width), lambda b, pt, ly: (b, 0)),
    )
    return pl.pallas_call(
        _attn_sample_kernel,
        grid_spec=grid_spec,
        out_shape=SDS((nb * t_len, width), BF16),
        compiler_params=_cparams(1),
        name="attn_sample",
    )(pt, lyr, cst, zb, zb, zb, lq1, lk1, lq2, lk2, gn,
      *([cache_k] * n_pages), *([cache_v] * n_pages))


def _hgrn_kernel(hq_ref, hf_ref, hi_ref, hg_ref, lb_ref, gn_ref, s0_ref, oh_ref, sout_ref,
                 s_sc, q_sc, k_sc, lf_sc, o_sc, *, per_seq):
    t = pl.program_id(1)
    nt = pl.num_programs(1)
    tt, w = hq_ref.shape
    dk = w // N_HEADS
    blk = _block_ones(w, dk, F32)
    ones_bf = jnp.where(blk, 1.0, 0.0).astype(BF16)

    def expand(s0):
        return jnp.where(blk, jnp.concatenate([s0] * N_HEADS, axis=1), 0.0)

    def contract(sf):
        acc = sf[:, 0:dk]
        for hh in range(1, N_HEADS):
            acc = acc + sf[:, hh * dk:(hh + 1) * dk]
        return acc

    if not per_seq:
        @pl.when(t == 0)
        def _():
            s_sc[...] = expand(s0_ref[...])

    lb = lb_ref[...]
    f = lb + (1.0 - lb) / (1.0 + jnp.exp(-hf_ref[...]))
    lf_sc[...] = jnp.log(jnp.maximum(f, HG_F_FLOOR))
    k_sc[...] = 1.0 - f
    q_sc[...] = _silu(hq_ref[...])
    row = lax.broadcasted_iota(jnp.int32, (CHUNK, 1), 0)

    def body(c, carry):
        sl = pl.ds(pl.multiple_of(c * CHUNK, CHUNK), CHUNK)
        q8, k8, v8 = q_sc[sl, :], k_sc[sl, :], hi_ref[sl, :]
        cum = lf_sc[sl, :]
        for s in (1, 2, 4):
            cum = cum + jnp.where(row >= s, pltpu.roll(cum, s, 0), 0.0)
        prods = []
        for s in range(CHUNK):
            keep = row >= s
            e = jnp.where(keep, jnp.exp(jnp.where(keep, cum - cum[s:s + 1, :], 0.0)), 0.0)
            prods.append(q8 * e * k8[s:s + 1, :])
        att = _dot(jnp.concatenate(prods, axis=0).astype(BF16), ones_bf)
        o = att[0:CHUNK, :] * v8[0:1, :]
        for s in range(1, CHUNK):
            o = o + att[s * CHUNK:(s + 1) * CHUNK, :] * v8[s:s + 1, :]
        s_old = expand(s0_ref[c]) if per_seq else s_sc[...]
        o = o + _dot_nt((q8 * jnp.exp(cum)).astype(BF16), s_old.astype(BF16))
        o_sc[sl, :] = o
        last = cum[CHUNK - 1:CHUNK, :]
        kd = (k8 * jnp.exp(last - cum)).astype(BF16).astype(F32)
        vb = v8.astype(BF16).astype(F32)
        upd = lax.dot_general(vb, kd, (((0,), (0,)), ((), ())), preferred_element_type=F32)
        s_new = jnp.exp(last) * s_old + jnp.where(blk, upd, 0.0)
        if per_seq:
            sout_ref[c] = contract(s_new)
        else:
            s_sc[...] = s_new
        return carry

    lax.fori_loop(0, tt // CHUNK, body, 0, unroll=math.gcd(4, tt // CHUNK))

    o = o_sc[...]
    o2 = o * o
    hi = o2.astype(BF16)
    lo = (o2 - hi.astype(F32)).astype(BF16)
    ms = (_dot(hi, ones_bf) + _dot(lo, ones_bf)) * (1.0 / dk)
    on = o * lax.rsqrt(ms + EPS) * gn_ref[...]
    oh_ref[...] = (on * _silu(hg_ref[...])).astype(BF16)

    if not per_seq:
        @pl.when(t == nt - 1)
        def _():
            sout_ref[...] = contract(s_sc[...])


def hgrn_mixer(z, lb, gn, s0, nb, t_len, tt):
    per_seq = t_len == CHUNK
    w = lb.shape[-1]
    dk = w // N_HEADS
    row = pl.BlockSpec((1, w), lambda b, t: (0, 0))
    if per_seq:
        sps = math.gcd(nb, 16)
        nt, tt, grid = 1, sps * t_len, (nb // sps, 1)
        st = pl.BlockSpec((sps, w, dk), lambda b, t: (b, 0, 0))
    else:
        nt = t_len // tt
        grid = (nb, nt)
        st = pl.BlockSpec((None, w, dk), lambda b, t: (b, 0, 0))
    col = lambda cidx: pl.BlockSpec((tt, w), lambda b, t: (b * nt + t, cidx))
    return pl.pallas_call(
        functools.partial(_hgrn_kernel, per_seq=per_seq),
        grid=grid,
        in_specs=[col(COL_HQ), col(COL_HF), col(COL_HI), col(COL_HG), row, row, st],
        out_specs=[pl.BlockSpec((tt, w), lambda b, t: (b * nt + t, 0)), st],
        out_shape=[SDS((nb * t_len, w), BF16), SDS((nb, w, dk), F32)],
        scratch_shapes=[pltpu.VMEM((w, w), F32), pltpu.VMEM((tt, w), F32), pltpu.VMEM((tt, w), F32),
                        pltpu.VMEM((tt, w), F32), pltpu.VMEM((tt, w), F32)],
        compiler_params=_cparams(2),
        name="hgrn",
    )(z, z, z, z, lb, gn, s0)


def _mem_attn_kernel(off_ref, q_ref, mk_ref, mv_ref, o_ref, *, kv_t, bps):
    w = q_ref.shape[1]
    tq = q_ref.shape[0] // bps
    dh = w // N_HEADS
    head = lax.broadcasted_iota(jnp.int32, (1, w), 1) // dh
    for e in range(bps):
        q = q_ref[e * tq:(e + 1) * tq, :].astype(BF16) * jnp.asarray(dh ** -0.5, BF16)
        mk = mk_ref[e * w:(e + 1) * w, :].astype(BF16)
        mv = mv_ref[e * w:(e + 1) * w, :].astype(BF16)
        out = jnp.zeros((tq, w), F32)
        for hh in range(N_HEADS):
            qh = jnp.where(head == hh, q, jnp.zeros_like(q))
            s = _dot(qh, mk) if kv_t else _dot_nt(qh, mk)
            p = jnp.exp(s - jnp.max(s, axis=-1, keepdims=True))
            p = (p / jnp.sum(p, axis=-1, keepdims=True)).astype(BF16)
            out = out + jnp.where(head == hh, _dot_nt(p, mv) if kv_t else _dot(p, mv), 0.0)
        o_ref[e * tq:(e + 1) * tq, :] = out.astype(BF16)


def mem_attn(off, z, mk, mv, k_col, v_col, nb, t_len, tq, kv_t=False, bps=1):
    nq = t_len // tq
    w = MEM_TOKENS
    grid_spec = pltpu.PrefetchScalarGridSpec(
        num_scalar_prefetch=1,
        grid=(nb // bps, nq),
        in_specs=[pl.BlockSpec((bps * tq, w), lambda b, i, off: (b * nq + i, COL_MQ)),
                  pl.BlockSpec((bps * MEM_TOKENS, w), lambda b, i, off: (off[0] + b, k_col)),
                  pl.BlockSpec((bps * MEM_TOKENS, w), lambda b, i, off: (off[0] + b, v_col))],
        out_specs=pl.BlockSpec((bps * tq, w), lambda b, i, off: (b * nq + i, 0)),
    )
    return pl.pallas_call(
        functools.partial(_mem_attn_kernel, kv_t=kv_t, bps=bps),
        grid_spec=grid_spec,
        out_shape=SDS((nb * t_len, w), BF16),
        compiler_params=_cparams(2),
        name="mem_attn",
    )(off, z, mk, mv)


def _merge_kernel(h_ref, ys_ref, oa_ref, oh_ref, om_ref, g0, g1, g2, g3, ws, wa, wh, wm, wo, o_ref):
    m = (g0[...] * _dot(ys_ref[...], ws[...]) + g1[...] * _dot(oa_ref[...], wa[...])
         + g2[...] * _dot(oh_ref[...], wh[...]) + g3[...] * _dot(om_ref[...], wm[...]))
    o_ref[...] = h_ref[...] + _dot(m.astype(BF16), wo[...])


def merge_out(h, ys, oa, oh, om, z, ws, wa, wh, wm, wo, tm):
    n = h.shape[0]
    tm = min(tm, n)
    rows = lambda a: pl.BlockSpec((tm, a.shape[1]), lambda i: (i, 0))
    full = lambda a: pl.BlockSpec(a.shape, lambda i: (0, 0))
    gate = lambda k: pl.BlockSpec((tm, D_MODEL), lambda i: (i, GATE_COL0 + k))
    return pl.pallas_call(
        _merge_kernel,
        grid=(n // tm,),
        in_specs=[rows(h), rows(ys), rows(oa), rows(oh), rows(om),
                  gate(0), gate(1), gate(2), gate(3),
                  full(ws), full(wa), full(wh), full(wm), full(wo)],
        out_specs=rows(h),
        out_shape=SDS(h.shape, F32),
        compiler_params=_cparams(1),
        name="merge_out",
    )(h, ys, oa, oh, om, z, z, z, z, ws, wa, wh, wm, wo)


def _ffn_kernel(h_ref, g_ref, wg_ref, wu_ref, wo_ref, gf_ref, o_ref, hn_sc, acc_sc, *, final):
    j = pl.program_id(1)
    nj = pl.num_programs(1)

    @pl.when(j == 0)
    def _():
        hn_sc[...] = _rms(h_ref[...], g_ref[...]).astype(BF16)
        acc_sc[...] = jnp.zeros(acc_sc.shape, F32)

    hn = hn_sc[...]
    act = _silu(_dot(hn, wg_ref[...])) * _dot(hn, wu_ref[...])
    acc_sc[...] += _dot(act.astype(BF16), wo_ref[...])

    @pl.when(j == nj - 1)
    def _():
        out = h_ref[...] + acc_sc[...]
        o_ref[...] = _rms(out, gf_ref[...]) if final else out


def ffn(h, g, w_in, w_out, gf, tm, tf, final):
    n = h.shape[0]
    tm = min(tm, n)
    nf = D_FF // tf
    return pl.pallas_call(
        functools.partial(_ffn_kernel, final=final),
        grid=(n // tm, nf),
        in_specs=[pl.BlockSpec((tm, D_MODEL), lambda i, j: (i, 0)),
                  pl.BlockSpec((1, D_MODEL), lambda i, j: (0, 0)),
                  pl.BlockSpec((D_MODEL, tf), lambda i, j: (0, j)),
                  pl.BlockSpec((D_MODEL, tf), lambda i, j: (0, nf + j)),
                  pl.BlockSpec((tf, D_MODEL), lambda i, j: (j, 0)),
                  pl.BlockSpec((1, D_MODEL), lambda i, j: (0, 0))],
        out_specs=pl.BlockSpec((tm, D_MODEL), lambda i, j: (i, 0)),
        out_shape=SDS(h.shape, F32),
        scratch_shapes=[pltpu.VMEM((tm, D_MODEL), BF16), pltpu.VMEM((tm, D_MODEL), F32)],
        compiler_params=_cparams(2),
        name="ffn",
    )(h, g, w_in, w_in, w_out, gf)


def _rope_tables(pos):
    inv = ROPE_THETA ** (-jnp.arange(0, 64, 2, dtype=F32) / 64)
    ang = pos[:, None] * inv[None, :]
    ang = jnp.concatenate([ang, ang, ang, ang], axis=-1)
    cos, sin = jnp.cos(ang), jnp.sin(ang)
    upper = (jnp.arange(HEAD_LANES) % 64) >= 32
    return cos, jnp.where(upper, sin, 0.0), jnp.where(upper, 0.0, -sin)


def _block_diag(a):
    g, r, c = a.shape
    eye = jnp.eye(g, dtype=a.dtype)
    return (a[:, :, None, :] * eye[:, None, :, None]).reshape(g * r, g * c)


def _s5_params(lam_re, lam_im, log_dt, b_re, b_im, c_re, c_im, d):
    dt = jnp.exp(log_dt.astype(F32))[:, None]
    lr, li = lam_re.astype(F32), lam_im.astype(F32)
    mag = jnp.exp(lr * dt)
    ar, ai = mag * jnp.cos(li * dt), mag * jnp.sin(li * dt)
    nr, ni = ar - 1.0, ai
    den = lr * lr + li * li
    cr, ci = (nr * lr + ni * li) / den, (ni * lr - nr * li) / den
    br, bi = b_re.astype(F32), b_im.astype(F32)
    bbr = cr[..., None] * br - ci[..., None] * bi
    bbi = cr[..., None] * bi + ci[..., None] * br
    ar, ai = ar.reshape(1, -1), ai.reshape(1, -1)
    pows_r, pows_i = [ar], [ai]
    for _ in range(CHUNK - 1):
        pr, pi = pows_r[-1], pows_i[-1]
        pows_r.append(pr * ar - pi * ai)
        pows_i.append(pr * ai + pi * ar)
    rep = lambda x: jnp.broadcast_to(x, (CHUNK, x.shape[-1]))
    pw = jnp.stack([rep(pows_r[0]), rep(pows_i[0]), rep(pows_r[1]), rep(pows_i[1]),
                    rep(pows_r[3]), rep(pows_i[3]),
                    jnp.concatenate(pows_r, axis=0), jnp.concatenate(pows_i, axis=0)])
    return {
        "bre": _block_diag(jnp.swapaxes(bbr, 1, 2)).astype(BF16),
        "bim": _block_diag(jnp.swapaxes(bbi, 1, 2)).astype(BF16),
        "cre": _block_diag(jnp.swapaxes(c_re.astype(F32), 1, 2)).astype(BF16),
        "cim": _block_diag(jnp.swapaxes(c_im.astype(F32), 1, 2)).astype(BF16),
        "d": d.astype(F32).reshape(1, -1),
        "pw": pw,
    }


def _state_to_rows(s):
    b, h, dk, dv = s.shape
    return jnp.swapaxes(s, 2, 3).reshape(b, h * dv, dk)


def _rows_to_state(s, h):
    b, hv, dk = s.shape
    return jnp.swapaxes(s.reshape(b, h, hv // h, dk), 2, 3)


def kernel(x_prompt, x_sample, mem_prompt, cache_k, cache_v, page_table, cache_mem_k, cache_mem_v, state_ssm_re, state_ssm_im, state_hgrn, norm_mix, w_in, ssm_lam_re, ssm_lam_im, ssm_log_dt, ssm_b_re, ssm_b_im, ssm_c_re, ssm_c_im, ssm_d, ssm_w_glu, att_lq1, att_lk1, att_lq2, att_lk2, att_norm, hg_gamma, hg_norm, mem_norm, w_mem_kv, wb_ssm, wb_att, wb_hg, wb_mem, w_out, norm_ffn, w_ffn_in, w_ffn_out, norm_final):
    depth = w_in.shape[0]
    b, s, _ = x_prompt.shape
    db, ds, _ = x_sample.shape
    n_pages, page = page_table.shape[1], cache_k.shape[2]
    past = n_pages * page
    n_phys = cache_k.shape[1]
    g_ssm, p_ssm = state_ssm_re.shape[2], state_ssm_re.shape[3]
    hgw = hg_gamma.shape[1]
    memw = w_mem_kv.shape[2] // 2

    u_w = ssm_b_re.shape[1] * ssm_b_re.shape[3]
    split = IN_WIDTH - N_HEADS * D_MODEL
    w_in_p = jnp.concatenate([w_in[:, :, u_w:split], w_in[:, :, :u_w], w_in[:, :, split:]],
                             axis=-1).astype(BF16)
    bf = lambda a: a.astype(BF16)
    w_glu, w_mkv = bf(ssm_w_glu), bf(w_mem_kv)
    wbs, wba, wbh, wbm, wo = bf(wb_ssm), bf(wb_att), bf(wb_hg), bf(wb_mem), bf(w_out)
    wfi, wfo = bf(w_ffn_in), bf(w_ffn_out)
    gam = jax.nn.softmax(hg_gamma.astype(F32), axis=0)
    lb_all = jnp.cumsum(gam, axis=0) - gam[0]

    tabs_p = _rope_tables(jnp.arange(s, dtype=F32))
    tabs_s = tuple(jnp.tile(t, (db, 1)) for t in _rope_tables(past + jnp.arange(ds, dtype=F32)))

    ck = cache_k.reshape(depth, n_phys, page * N_HEADS, HEAD_LANES)
    cv = cache_v.reshape(depth, n_phys, page * N_HEADS, HEAD_LANES)
    cmk = jnp.transpose(cache_mem_k, (0, 1, 3, 4, 2)).reshape(depth * db * memw, MEM_TOKENS)
    cmv = jnp.transpose(cache_mem_v, (0, 1, 3, 4, 2)).reshape(depth * db * memw, MEM_TOKENS)
    bps = math.gcd(db, 8)
    memp = mem_prompt.reshape(b * MEM_TOKENS, D_MODEL)

    hp = x_prompt.reshape(b * s, D_MODEL)
    hs = x_sample.reshape(db * ds, D_MODEL)
    zeros_p = jnp.zeros((b, 1, g_ssm * p_ssm), F32)
    zeros_hg = jnp.zeros((b, hgw, hgw // N_HEADS), F32)

    tm_p = min(1024, s)
    tm_s = min(1024, db * ds)
    tt_p = min(512, s)
    tq_p = min(512, s)

    outs = {k: [] for k in ("kp", "vp", "mkp", "mvp", "srp", "sip", "hgp", "ks", "vs", "srs", "sis", "hgs")}
    for l in range(depth):
        lam_init = 0.8 - 0.6 * math.exp(-0.3 * l)
        cst = jnp.full((1,), lam_init, F32)
        lyr = jnp.full((1,), l, jnp.int32)
        row = lambda a: a[l].astype(F32).reshape(1, -1)
        s5p = _s5_params(ssm_lam_re[l], ssm_lam_im[l], ssm_log_dt[l], ssm_b_re[l], ssm_b_im[l],
                         ssm_c_re[l], ssm_c_im[l], ssm_d[l])
        lq1, lk1, lq2, lk2 = row(att_lq1), row(att_lk1), row(att_lq2), row(att_lk2)
        gn_att = row(att_norm)
        gn_hg = jnp.tile(row(hg_norm), (1, N_HEADS))
        lb = lb_all[l].reshape(1, -1)
        final = l == depth - 1
        gfin = norm_final.astype(F32).reshape(1, -1)

        mkv = mem_kv_proj(memp, row(mem_norm), w_mkv[l])
        z, zb, k_rows, v_rows = in_proj(hp, row(norm_mix), w_in_p[l], *tabs_p, tm_p)
        ys, sr, si = s5_mixer(z, zeros_p, zeros_p, s5p, w_glu[l], b, s, tt_p)
        oa = attn_prompt(cst, zb, _value_rows(zb, b, s), lq1, lk1, lq2, lk2, gn_att, b, s, tq_p)
        oh, sh = hgrn_mixer(z, lb, gn_hg, zeros_hg, b, s, tt_p)
        om = mem_attn(jnp.zeros((1,), jnp.int32), z, mkv, mkv, 0, 1, b, s, tt_p)
        hp = merge_out(hp, ys, oa, oh, om, z, wbs[l], wba[l], wbh[l], wbm[l], wo[l], 256)
        hp = ffn(hp, row(norm_ffn), wfi[l], wfo[l], gfin, 512, D_FF // 2, final)
        outs["kp"].append(k_rows.reshape(b, s, N_HEADS, HEAD_LANES))
        outs["vp"].append(v_rows.reshape(b, s, N_HEADS, HEAD_LANES))
        outs["mkp"].append(mkv[:, :memw].reshape(b, MEM_TOKENS, N_HEADS, memw // N_HEADS))
        outs["mvp"].append(mkv[:, memw:].reshape(b, MEM_TOKENS, N_HEADS, memw // N_HEADS))
        outs["srp"].append(sr.reshape(b, g_ssm, p_ssm))
        outs["sip"].append(si.reshape(b, g_ssm, p_ssm))
        outs["hgp"].append(_rows_to_state(sh, N_HEADS))

        z, zb, k_rows, v_rows = in_proj(hs, row(norm_mix), w_in_p[l], *tabs_s, tm_s)
        ys, sr, si = s5_mixer(z, state_ssm_re[l].reshape(db, 1, -1), state_ssm_im[l].reshape(db, 1, -1),
                              s5p, w_glu[l], db, ds, ds)
        oa = attn_sample(page_table, lyr, cst, zb, ck, cv, lq1, lk1, lq2, lk2, gn_att, ds)
        oh, sh = hgrn_mixer(z, lb, gn_hg, _state_to_rows(state_hgrn[l]), db, ds, ds)
        om = mem_attn(jnp.full((1,), l * db // bps, jnp.int32), z, cmk, cmv, 0, 0, db, ds, ds,
                      kv_t=True, bps=bps)
        hs = merge_out(hs, ys, oa, oh, om, z, wbs[l], wba[l], wbh[l], wbm[l], wo[l], 256)
        hs = ffn(hs, row(norm_ffn), wfi[l], wfo[l], gfin, 512, D_FF // 2, final)
        outs["ks"].append(k_rows.reshape(db, ds, N_HEADS, HEAD_LANES))
        outs["vs"].append(v_rows.reshape(db, ds, N_HEADS, HEAD_LANES))
        outs["srs"].append(sr.reshape(db, g_ssm, p_ssm))
        outs["sis"].append(si.reshape(db, g_ssm, p_ssm))
        outs["hgs"].append(_rows_to_state(sh, N_HEADS))

    st = jnp.stack
    return (hp.reshape(b, s, D_MODEL), hs.reshape(db, ds, D_MODEL),
            st(outs["kp"]), st(outs["vp"]), st(outs["mkp"]), st(outs["mvp"]),
            st(outs["srp"]), st(outs["sip"]), st(outs["hgp"]),
            st(outs["ks"]), st(outs["vs"]), st(outs["srs"]), st(outs["sis"]), st(outs["hgs"]))
```

```python
import functools
import math

import jax
import jax.numpy as jnp
from jax import lax
from jax.experimental import pallas as pl
from jax.experimental.pallas import tpu as pltpu

F32 = jnp.float32
BF16 = jnp.bfloat16
SDS = jax.ShapeDtypeStruct

D_MODEL = 1024
EPS = 1e-6
ROPE_THETA = 10000.0
MASK_VALUE = -1e30
HG_F_FLOOR = 1e-20
N_HEADS = 4
HEAD_LANES = 128
MEM_TOKENS = 256
D_FF = 2816
IN_WIDTH = 7168
IN_TILE = 512
CHUNK = 8
HG_CHUNK = 16
LOG2E = math.log2(math.e)
Q_SCALE = 64 ** -0.5 * math.log2(math.e)
VT_ROWS = HEAD_LANES + 16
VMEM_LIMIT = 56 * 1024 * 1024

COL_HQ, COL_HF, COL_HI, COL_HG, COL_MQ, COL_U = 6, 7, 8, 9, 10, 11
GATE_COL0 = 3


def _cparams(n_axes):
    return pltpu.CompilerParams(dimension_semantics=("arbitrary",) * n_axes,
                                vmem_limit_bytes=VMEM_LIMIT)


def _sigmoid(x):
    return 0.5 * jnp.tanh(0.5 * x) + 0.5


def _silu(x):
    return x * _sigmoid(x)


def _rms(x, g):
    return x * lax.rsqrt(jnp.mean(x * x, axis=-1, keepdims=True) + EPS) * g


def _dot(a, b):
    return jnp.dot(a, b, preferred_element_type=F32)


def _dot_nt(a, b):
    return lax.dot_general(a, b, (((1,), (1,)), ((), ())), preferred_element_type=F32)


def _block_ones(n, blk, dtype):
    r = lax.broadcasted_iota(jnp.int32, (n, n), 0) // blk
    c = lax.broadcasted_iota(jnp.int32, (n, n), 1) // blk
    return r == c


def _inproj_kernel(x_ref, g_ref, w_ref, cos_ref, sa_ref, sb_ref, z_ref, zb_ref, kr_ref, vr_ref, *rest):
    xn_ref, vt_ref = rest[-1], (rest[0] if len(rest) == 2 else None)
    j = pl.program_id(1)
    tm = x_ref.shape[0]
    half = IN_TILE // 2

    @pl.when(j == 0)
    def _():
        xn_ref[...] = _rms(x_ref[...], g_ref[...]).astype(BF16)

    def halves():
        for c in range(2):
            yield c, slice(c * half, (c + 1) * half), _dot(xn_ref[...], w_ref[:, c * half:(c + 1) * half])

    def rotary(scale, rows_ref):
        cos, sa, sb = cos_ref[...], sa_ref[...], sb_ref[...]
        for c, _, acc in halves():
            for hh in range(2 * c, 2 * c + 2):
                sl = slice(hh * HEAD_LANES, (hh + 1) * HEAD_LANES)
                x = acc[:, (hh - 2 * c) * HEAD_LANES:(hh - 2 * c + 1) * HEAD_LANES]
                r = (x * cos + pltpu.roll(x, 32, 1) * sa + pltpu.roll(x, 96, 1) * sb)
                z_ref[:, sl] = r
                zb_ref[:, sl] = (r * scale).astype(BF16) if scale is not None else r.astype(BF16)
                if rows_ref is not None:
                    rows_ref[pl.ds(hh, tm, stride=N_HEADS), :] = r

    @pl.when(j == 0)
    def _():
        rotary(Q_SCALE, None)

    @pl.when(j == 1)
    def _():
        rotary(None, kr_ref)

    @pl.when(j == 2)
    def _():
        for c, sl, acc in halves():
            z_ref[:, sl] = acc
            zb_ref[:, sl] = acc.astype(BF16)
            for hh in range(2 * c, 2 * c + 2):
                vh = acc[:, (hh - 2 * c) * HEAD_LANES:(hh - 2 * c + 1) * HEAD_LANES]
                vr_ref[pl.ds(hh, tm, stride=N_HEADS), :] = vh
                if vt_ref is not None:
                    extra = lax.broadcasted_iota(jnp.int32, (VT_ROWS - HEAD_LANES, tm), 0) == 0
                    vt_ref[hh, 0:HEAD_LANES, :] = vh.T.astype(BF16)
                    vt_ref[hh, HEAD_LANES:, :] = jnp.where(extra, 1.0, 0.0).astype(BF16)

    @pl.when(jnp.logical_and(j > 2, j < 6))
    def _():
        for _, sl, acc in halves():
            z_ref[:, sl] = acc

    @pl.when(j >= 6)
    def _():
        for _, sl, acc in halves():
            z_ref[:, sl] = _sigmoid(acc)


def in_proj(h, g, w, cos_t, sa_t, sb_t, tm, t_len=None):
    n = h.shape[0]
    nper = cos_t.shape[0] // tm
    tab = pl.BlockSpec((tm, HEAD_LANES), lambda i, j: (i % nper, 0))
    rows = pl.BlockSpec((tm * N_HEADS, HEAD_LANES), lambda i, j: (i, 0))
    vt_specs, vt_shapes = [], []
    if t_len is not None:
        per_seq = t_len // tm
        vt_specs = [pl.BlockSpec((N_HEADS, VT_ROWS, tm), lambda i, j: (i // per_seq, 0, i % per_seq))]
        vt_shapes = [SDS((n // t_len * N_HEADS, VT_ROWS, t_len), BF16)]
    return pl.pallas_call(
        _inproj_kernel,
        grid=(n // tm, IN_WIDTH // IN_TILE),
        in_specs=[pl.BlockSpec((tm, D_MODEL), lambda i, j: (i, 0)),
                  pl.BlockSpec((1, D_MODEL), lambda i, j: (0, 0)),
                  pl.BlockSpec((D_MODEL, IN_TILE), lambda i, j: (0, j)),
                  tab, tab, tab],
        out_specs=[pl.BlockSpec((tm, IN_TILE), lambda i, j: (i, j)),
                   pl.BlockSpec((tm, IN_TILE), lambda i, j: (i, jnp.minimum(j, 2))),
                   rows, rows] + vt_specs,
        out_shape=[SDS((n, IN_WIDTH), F32), SDS((n, 3 * IN_TILE), BF16),
                   SDS((n * N_HEADS, HEAD_LANES), F32), SDS((n * N_HEADS, HEAD_LANES), F32)] + vt_shapes,
        scratch_shapes=[pltpu.VMEM((tm, D_MODEL), BF16)],
        compiler_params=_cparams(2),
        name="in_proj",
    )(h, g, w, cos_t, sa_t, sb_t)


def _memkv_kernel(x_ref, g_ref, w_ref, o_ref):
    o_ref[...] = _dot(_rms(x_ref[...], g_ref[...]).astype(BF16), w_ref[...])


def mem_kv_proj(x, g, w):
    n, wd = x.shape[0], w.shape[1]
    tm = 256
    return pl.pallas_call(
        _memkv_kernel,
        grid=(n // tm,),
        in_specs=[pl.BlockSpec((tm, D_MODEL), lambda i: (i, 0)),
                  pl.BlockSpec((1, D_MODEL), lambda i: (0, 0)),
                  pl.BlockSpec((D_MODEL, wd), lambda i: (0, 0))],
        out_specs=pl.BlockSpec((tm, wd), lambda i: (i, 0)),
        out_shape=SDS((n, wd), F32),
        compiler_params=_cparams(1),
        name="mem_kv_proj",
    )(x, g, w)


def _s5_kernel(u_ref, h0r_ref, h0i_ref, bre_ref, bim_ref, cre_ref, cim_ref, d_ref, wg_ref,
               pw_ref, ys_ref, sr_ref, si_ref, xr_sc, xi_sc, hr_sc, hi_sc, *, per_seq):
    t = pl.program_id(1)
    tt = u_ref.shape[0]
    ns = xr_sc.shape[1]
    n_chunks = tt // CHUNK
    u = u_ref[...]
    ub = u.astype(BF16)
    xr_sc[...] = _dot(ub, bre_ref[...])
    xi_sc[...] = _dot(ub, bim_ref[...])
    row = lax.broadcasted_iota(jnp.int32, (CHUNK, 1), 0)
    apr, api = pw_ref[6], pw_ref[7]

    def scan_chunk(c, hr, hi):
        sl = pl.ds(pl.multiple_of(c * CHUNK, CHUNK), CHUNK)
        xr, xi = xr_sc[sl, :], xi_sc[sl, :]
        for k, s in enumerate((1, 2, 4)):
            ar, ai = pw_ref[2 * k], pw_ref[2 * k + 1]
            keep = row >= s
            pr = jnp.where(keep, pltpu.roll(xr, s, 0), 0.0)
            pi = jnp.where(keep, pltpu.roll(xi, s, 0), 0.0)
            xr, xi = xr + (ar * pr - ai * pi), xi + (ar * pi + ai * pr)
        xr, xi = xr + (apr * hr - api * hi), xi + (apr * hi + api * hr)
        xr_sc[sl, :] = xr
        xi_sc[sl, :] = xi
        return xr, xi

    if per_seq:
        def body(c, carry):
            one = pl.ds(c, 1)
            xr, xi = scan_chunk(c, jnp.broadcast_to(h0r_ref[one, :], (CHUNK, ns)),
                                jnp.broadcast_to(h0i_ref[one, :], (CHUNK, ns)))
            sr_ref[one, :] = xr[CHUNK - 1:CHUNK, :]
            si_ref[one, :] = xi[CHUNK - 1:CHUNK, :]
            return carry

        lax.fori_loop(0, n_chunks, body, 0, unroll=math.gcd(4, n_chunks))
    else:
        @pl.when(t == 0)
        def _():
            hr_sc[...] = jnp.broadcast_to(h0r_ref[...], hr_sc.shape)
            hi_sc[...] = jnp.broadcast_to(h0i_ref[...], hi_sc.shape)

        def body(c, carry):
            xr, xi = scan_chunk(c, *carry)
            return (jnp.broadcast_to(xr[CHUNK - 1:CHUNK, :], xr.shape),
                    jnp.broadcast_to(xi[CHUNK - 1:CHUNK, :], xi.shape))

        hr, hi = lax.fori_loop(0, n_chunks, body, (hr_sc[...], hi_sc[...]),
                               unroll=math.gcd(4, n_chunks))
        hr_sc[...] = hr
        hi_sc[...] = hi
        sr_ref[...] = hr[0:1, :]
        si_ref[...] = hi[0:1, :]

    y = (_dot(xr_sc[...].astype(BF16), cre_ref[...]) - _dot(xi_sc[...].astype(BF16), cim_ref[...])
         + d_ref[...] * u)
    ys = jax.nn.gelu(y)
    gate = _sigmoid(_dot(ys.astype(BF16), wg_ref[...]))
    ys_ref[...] = (ys * gate).astype(BF16)


def s5_mixer(z, h0r, h0i, prm, wglu, nb, t_len, tt):
    per_seq = t_len == CHUNK
    ns = h0r.shape[-1]
    width = prm["d"].shape[-1]
    full = lambda a: pl.BlockSpec(a.shape, lambda b, t: (0,) * a.ndim)
    if per_seq:
        grid, tt = (1, 1), nb * t_len
        h0r, h0i = h0r.reshape(nb, ns), h0i.reshape(nb, ns)
        st = pl.BlockSpec((nb, ns), lambda b, t: (0, 0))
        rows = lambda cidx: pl.BlockSpec((tt, width), lambda b, t: (0, cidx))
    else:
        nt = t_len // tt
        grid = (nb, nt)
        st = pl.BlockSpec((None, 1, ns), lambda b, t: (b, 0, 0))
        rows = lambda cidx: pl.BlockSpec((tt, width), lambda b, t: (b * nt + t, cidx))
    return pl.pallas_call(
        functools.partial(_s5_kernel, per_seq=per_seq),
        grid=grid,
        in_specs=[rows(COL_U), st, st,
                  full(prm["bre"]), full(prm["bim"]), full(prm["cre"]), full(prm["cim"]),
                  full(prm["d"]), full(wglu), full(prm["pw"])],
        out_specs=[rows(0), st, st],
        out_shape=[SDS((nb * t_len, width), BF16), SDS(h0r.shape, F32), SDS(h0r.shape, F32)],
        scratch_shapes=[pltpu.VMEM((tt, ns), F32), pltpu.VMEM((tt, ns), F32),
                        pltpu.VMEM((CHUNK, ns), F32), pltpu.VMEM((CHUNK, ns), F32)],
        compiler_params=_cparams(2),
        name="s5",
    )(z, h0r, h0i, prm["bre"], prm["bim"], prm["cre"], prm["cim"], prm["d"], wglu, prm["pw"])


def _lambda(lq1, lk1, lq2, lk2, lam_init):
    return (jnp.exp(jnp.sum(lq1[...] * lk1[...], axis=-1, keepdims=True))
            - jnp.exp(jnp.sum(lq2[...] * lk2[...], axis=-1, keepdims=True)) + lam_init)


def _two_maps(q):
    lane = lax.broadcasted_iota(jnp.int32, (1, HEAD_LANES), 1)
    zero = jnp.zeros_like(q)
    return jnp.concatenate([jnp.where(lane < 64, q, zero), jnp.where(lane >= 64, q, zero)], axis=0)


def _diff_finish(o1, o2, lam, lam_init, gn):
    return _rms(o1 - lam * o2, gn) * (1.0 - lam_init)


def _attn_prompt_kernel(cst_ref, q_ref, k_ref, vt_ref, lq1, lk1, lq2, lk2, gn_ref, o_ref,
                        m_sc, acc_sc, s0_sc, s1_sc, b0_sc, b1_sc):
    qi = pl.program_id(2)
    tq = q_ref.shape[0]
    w = 2 * tq
    lam_init = cst_ref[0]
    qq = _two_maps(q_ref[...])
    m_sc[...] = jnp.full(m_sc.shape, MASK_VALUE, F32)
    acc_sc[...] = jnp.zeros(acc_sc.shape, F32)
    slots = ((s0_sc, b0_sc), (s1_sc, b1_sc))

    def scores(kb, slot):
        sl = pl.ds(pl.multiple_of(kb * tq, tq), tq)
        s = _dot_nt(k_ref[sl, :], qq)
        slots[slot][0][...] = s
        slots[slot][1][...] = jnp.max(s, axis=0, keepdims=True)

    def consume(kb, slot, masked):
        sl = pl.ds(pl.multiple_of(kb * tq, tq), tq)
        s = slots[slot][0][...]
        if masked:
            key = lax.broadcasted_iota(jnp.int32, (tq, w), 0)
            qry = lax.broadcasted_iota(jnp.int32, (tq, w), 1) % tq
            s = jnp.where(key <= qry, s, MASK_VALUE)
            bmax = jnp.max(s, axis=0, keepdims=True)
        else:
            bmax = slots[slot][1][...]
        m_prev = m_sc[...]
        m_new = jnp.maximum(m_prev, bmax)
        p = jnp.exp2(s - m_new)
        acc_sc[...] = jnp.exp2(m_prev - m_new) * acc_sc[...] + _dot(vt_ref[:, sl], p.astype(BF16))
        m_sc[...] = m_new

    def body(j, carry):
        scores(2 * j + 1, 1)
        consume(2 * j, 0, False)
        scores(2 * j + 2, 0)
        consume(2 * j + 1, 1, False)
        return carry

    scores(0, 0)
    lax.fori_loop(0, qi // 2, body, 0)

    @pl.when(qi % 2 == 0)
    def _():
        consume(qi, 0, True)

    @pl.when(qi % 2 == 1)
    def _():
        scores(qi, 1)
        consume(qi - 1, 0, False)
        consume(qi, 1, True)

    lam = _lambda(lq1, lk1, lq2, lk2, lam_init)
    acc = acc_sc[...]
    ot = acc[:HEAD_LANES, :] / acc[HEAD_LANES:HEAD_LANES + 1, :]
    o_ref[...] = _diff_finish(ot[:, :tq].T, ot[:, tq:].T, lam, lam_init, gn_ref[...]).astype(BF16)


def attn_prompt(cst, zb, vt, lq1, lk1, lq2, lk2, gn, nb, t_len, tq):
    nq = t_len // tq
    vec = pl.BlockSpec((1, 64), lambda b, h, i: (0, 0))
    return pl.pallas_call(
        _attn_prompt_kernel,
        grid=(nb, N_HEADS, nq),
        in_specs=[pl.BlockSpec(memory_space=pltpu.SMEM),
                  pl.BlockSpec((tq, HEAD_LANES), lambda b, h, i: (b * nq + i, h)),
                  pl.BlockSpec((t_len, HEAD_LANES), lambda b, h, i: (b, N_HEADS + h)),
                  pl.BlockSpec((VT_ROWS, t_len), lambda b, h, i: (b * N_HEADS + h, 0)),
                  vec, vec, vec, vec,
                  pl.BlockSpec((1, HEAD_LANES), lambda b, h, i: (0, 0))],
        out_specs=pl.BlockSpec((tq, HEAD_LANES), lambda b, h, i: (b * nq + i, h)),
        out_shape=SDS((nb * t_len, N_HEADS * HEAD_LANES), BF16),
        scratch_shapes=[pltpu.VMEM((1, 2 * tq), F32), pltpu.VMEM((VT_ROWS, 2 * tq), F32),
                        pltpu.VMEM((tq, 2 * tq), F32), pltpu.VMEM((tq, 2 * tq), F32),
                        pltpu.VMEM((1, 2 * tq), F32), pltpu.VMEM((1, 2 * tq), F32)],
        compiler_params=_cparams(3),
        name="attn_prompt",
    )(cst, zb, zb, vt, lq1, lk1, lq2, lk2, gn)


def _attn_sample_kernel(pt_ref, lyr_ref, cst_ref, q_ref, kn_ref, vn_ref, lq1, lk1, lq2, lk2, gn_ref,
                        *rest):
    n_pages = (len(rest) - 1) // 2
    k_refs, v_refs, o_ref = rest[:n_pages], rest[n_pages:2 * n_pages], rest[-1]
    t = q_ref.shape[0]
    lam_init = cst_ref[0]
    lam = _lambda(lq1, lk1, lq2, lk2, lam_init)
    hsl = [slice(hh * HEAD_LANES, (hh + 1) * HEAD_LANES) for hh in range(N_HEADS)]
    qh = [_two_maps(q_ref[:, sl]) for sl in hsl]
    qq = jnp.concatenate(qh, axis=0)
    rows, cols = N_HEADS * 2 * t, k_refs[0].shape[0]
    same_head = (lax.broadcasted_iota(jnp.int32, (rows, cols), 0) // (2 * t)
                 == lax.broadcasted_iota(jnp.int32, (rows, cols), 1) % N_HEADS)
    s_past = [jnp.where(same_head, _dot_nt(qq, k_refs[p][...].astype(BF16)), MASK_VALUE)
              for p in range(n_pages)]
    r = lax.broadcasted_iota(jnp.int32, (2 * t, t), 0) % t
    c = lax.broadcasted_iota(jnp.int32, (2 * t, t), 1)
    s_new = jnp.concatenate([jnp.where(c <= r, _dot_nt(qh[hh], kn_ref[:, hsl[hh]]), MASK_VALUE)
                             for hh in range(N_HEADS)], axis=0)
    m = s_past[0]
    for sp in s_past[1:]:
        m = jnp.maximum(m, sp)
    m = jnp.maximum(jnp.max(m, axis=-1, keepdims=True), jnp.max(s_new, axis=-1, keepdims=True))
    p_new = jnp.exp2(s_new - m)
    l = jnp.sum(p_new, axis=-1, keepdims=True)
    acc = jnp.concatenate([_dot(p_new[hh * 2 * t:(hh + 1) * 2 * t].astype(BF16), vn_ref[:, hsl[hh]])
                           for hh in range(N_HEADS)], axis=0)
    lsum = jnp.zeros(s_past[0].shape, F32)
    for p in range(n_pages):
        pp = jnp.exp2(s_past[p] - m)
        lsum = lsum + pp
        acc = acc + _dot(pp.astype(BF16), v_refs[p][...].astype(BF16))
    o = acc / (l + jnp.sum(lsum, axis=-1, keepdims=True))
    for hh in range(N_HEADS):
        o1 = o[hh * 2 * t:hh * 2 * t + t]
        o2 = o[hh * 2 * t + t:(hh + 1) * 2 * t]
        o_ref[:, hsl[hh]] = _diff_finish(o1, o2, lam, lam_init, gn_ref[...]).astype(BF16)


def attn_sample(page_table, lyr, cst, zb, cache_k, cache_v, lq1, lk1, lq2, lk2, gn, t_len):
    nb, n_pages = page_table.shape
    page_rows = cache_k.shape[2]
    width = N_HEADS * HEAD_LANES
    pt = page_table.reshape(-1)
    vec = pl.BlockSpec((1, 64), lambda b, pt, ly: (0, 0))

    def page_spec(p):
        return pl.BlockSpec((None, None, page_rows, HEAD_LANES),
                            lambda b, pt, ly: (ly[0], pt[b * n_pages + p], 0, 0))

    pages = [page_spec(p) for p in range(n_pages)]
    grid_spec = pltpu.PrefetchScalarGridSpec(
        num_scalar_prefetch=2,
        grid=(nb,),
        in_specs=[pl.BlockSpec(memory_space=pltpu.SMEM),
                  pl.BlockSpec((t_len, width), lambda b, pt, ly: (b, 0)),
                  pl.BlockSpec((t_len, width), lambda b, pt, ly: (b, 1)),
                  pl.BlockSpec((t_len, width), lambda b, pt, ly: (b, 2)),
                  vec, vec, vec, vec,
                  pl.BlockSpec((1, HEAD_LANES), lambda b, pt, ly: (0, 0))] + pages + pages,
        out_specs=pl.BlockSpec((t_len, width), lambda b, pt, ly: (b, 0)),
    )
    return pl.pallas_call(
        _attn_sample_kernel,
        grid_spec=grid_spec,
        out_shape=SDS((nb * t_len, width), BF16),
        compiler_params=_cparams(1),
        name="attn_sample",
    )(pt, lyr, cst, zb, zb, zb, lq1, lk1, lq2, lk2, gn,
      *([cache_k] * n_pages), *([cache_v] * n_pages))


def _hgrn_kernel(hq_ref, hf_ref, hi_ref, hg_ref, lb_ref, gn_ref, s0_ref, oh_ref, sout_ref,
                 s_sc, u_sc, hist_sc, dec_sc, o_sc, *, per_seq, chunk):
    t = pl.program_id(1)
    nt = pl.num_programs(1)
    tt, w = hq_ref.shape
    dk = w // N_HEADS
    n = tt // chunk
    blk = _block_ones(w, dk, F32)
    ones_bf = jnp.where(blk, 1.0, 0.0).astype(BF16)

    def expand(s0):
        return jnp.where(blk, jnp.concatenate([s0] * N_HEADS, axis=1), 0.0)

    def contract(sf):
        acc = sf[:, 0:dk]
        for hh in range(1, N_HEADS):
            acc = acc + sf[:, hh * dk:(hh + 1) * dk]
        return acc

    if not per_seq:
        @pl.when(t == 0)
        def _():
            s_sc[...] = expand(s0_ref[...])

    lb = lb_ref[...]
    f = lb + (1.0 - lb) / (1.0 + jnp.exp(-hf_ref[...]))
    k = 1.0 - f
    q = _silu(hq_ref[...])
    v = hi_ref[...]
    row = lax.broadcasted_iota(jnp.int32, (tt, 1), 0) % chunk
    cum = jnp.log(jnp.maximum(f, HG_F_FLOOR))
    s = 1
    while s < chunk:
        cum = cum + jnp.where(row >= s, pltpu.roll(cum, s, 0), 0.0)
        s *= 2
    cum2 = cum * LOG2E

    def chunk_row(x, s):
        x3 = x.reshape(n, chunk, w)
        return jnp.broadcast_to(x3[:, s:s + 1, :], (n, chunk, w)).reshape(tt, w)

    def tail(x, j):
        rows = chunk - j * CHUNK
        return x.reshape(n, chunk, x.shape[-1])[:, j * CHUNK:, :].reshape(n * rows, x.shape[-1])

    groups = chunk // CHUNK
    o_parts = [None] * groups
    for s in range(chunk):
        j = s // CHUNK
        keep = tail(row, j) >= s
        e = jnp.where(keep, jnp.exp2(jnp.minimum(tail(cum2, j) - tail(chunk_row(cum2, s), j), 0.0)), 0.0)
        att = _dot((tail(q, j) * e * tail(chunk_row(k, s), j)).astype(BF16), ones_bf)
        term = att * tail(chunk_row(v, s), j)
        o_parts[j] = term if o_parts[j] is None else o_parts[j] + term
    o3 = None
    for j in range(groups):
        pj = o_parts[j].reshape(n, chunk - j * CHUNK, w)
        if j:
            pj = jnp.concatenate([jnp.zeros((n, j * CHUNK, w), F32), pj], axis=1)
        o3 = pj if o3 is None else o3 + pj
    o_sc[...] = o3.reshape(tt, w)
    last = chunk_row(cum, chunk - 1)
    qd = (q * jnp.exp(cum)).astype(BF16)
    dec_sc[...] = jnp.exp(last)

    vb = v.astype(BF16)
    kb = (k * jnp.exp(last - cum)).astype(BF16)
    for c in range(n):
        sl = slice(c * chunk, (c + 1) * chunk)
        lhs, rhs = (vb[sl], kb[sl]) if chunk >= 16 else (vb[sl].astype(F32), kb[sl].astype(F32))
        upd = lax.dot_general(lhs, rhs, (((0,), (0,)), ((), ())), preferred_element_type=F32)
        u_sc[c] = jnp.where(blk, upd, 0.0)

    if per_seq:
        for c in range(n):
            s_old = expand(s0_ref[c])
            hist_sc[c] = s_old.astype(BF16)
            sout_ref[c] = contract(dec_sc[c * chunk:c * chunk + 1, :] * s_old + u_sc[c])
    else:
        def body(c, s_old):
            hist_sc[c] = s_old.astype(BF16)
            return dec_sc[pl.ds(pl.multiple_of(c * chunk, chunk), 1), :] * s_old + u_sc[c]

        s_sc[...] = lax.fori_loop(0, n, body, s_sc[...], unroll=math.gcd(4, n))

    for c in range(n):
        sl = slice(c * chunk, (c + 1) * chunk)
        o_sc[sl, :] = o_sc[sl, :] + _dot_nt(qd[sl], hist_sc[c])

    o = o_sc[...]
    o2 = o * o
    hi = o2.astype(BF16)
    lo = (o2 - hi.astype(F32)).astype(BF16)
    ms = (_dot(hi, ones_bf) + _dot(lo, ones_bf)) * (1.0 / dk)
    on = o * lax.rsqrt(ms + EPS) * gn_ref[...]
    oh_ref[...] = (on * _silu(hg_ref[...])).astype(BF16)

    if not per_seq:
        @pl.when(t == nt - 1)
        def _():
            sout_ref[...] = contract(s_sc[...])


def hgrn_mixer(z, lb, gn, s0, nb, t_len, tt):
    per_seq = t_len == CHUNK
    w = lb.shape[-1]
    dk = w // N_HEADS
    row = pl.BlockSpec((1, w), lambda b, t: (0, 0))
    if per_seq:
        sps = math.gcd(nb, 16)
        nt, tt, grid, chunk = 1, sps * t_len, (nb // sps, 1), t_len
        st = pl.BlockSpec((sps, w, dk), lambda b, t: (b, 0, 0))
    else:
        nt = t_len // tt
        grid = (nb, nt)
        chunk = math.gcd(HG_CHUNK, tt)
        st = pl.BlockSpec((None, w, dk), lambda b, t: (b, 0, 0))
    col = lambda cidx: pl.BlockSpec((tt, w), lambda b, t: (b * nt + t, cidx))
    return pl.pallas_call(
        functools.partial(_hgrn_kernel, per_seq=per_seq, chunk=chunk),
        grid=grid,
        in_specs=[col(COL_HQ), col(COL_HF), col(COL_HI), col(COL_HG), row, row, st],
        out_specs=[pl.BlockSpec((tt, w), lambda b, t: (b * nt + t, 0)), st],
        out_shape=[SDS((nb * t_len, w), BF16), SDS((nb, w, dk), F32)],
        scratch_shapes=[pltpu.VMEM((w, w), F32), pltpu.VMEM((tt // chunk, w, w), F32),
                        pltpu.VMEM((tt // chunk, w, w), BF16),
                        pltpu.VMEM((tt, w), F32), pltpu.VMEM((tt, w), F32)],
        compiler_params=_cparams(2),
        name="hgrn",
    )(z, z, z, z, lb, gn, s0)


def _mem_attn_kernel(off_ref, q_ref, mk_ref, mv_ref, o_ref, *, kv_t, bps):
    w = q_ref.shape[1]
    tq = q_ref.shape[0] // bps
    dh = w // N_HEADS
    head = lax.broadcasted_iota(jnp.int32, (1, w), 1) // dh
    for e in range(bps):
        q = q_ref[e * tq:(e + 1) * tq, :].astype(BF16) * jnp.asarray(dh ** -0.5, BF16)
        mk = mk_ref[e * w:(e + 1) * w, :].astype(BF16)
        mv = mv_ref[e * w:(e + 1) * w, :].astype(BF16)
        out = jnp.zeros((tq, w), F32)
        for hh in range(N_HEADS):
            qh = jnp.where(head == hh, q, jnp.zeros_like(q))
            s = _dot(qh, mk) if kv_t else _dot_nt(qh, mk)
            p = jnp.exp(s - jnp.max(s, axis=-1, keepdims=True))
            p = (p / jnp.sum(p, axis=-1, keepdims=True)).astype(BF16)
            out = out + jnp.where(head == hh, _dot_nt(p, mv) if kv_t else _dot(p, mv), 0.0)
        o_ref[e * tq:(e + 1) * tq, :] = out.astype(BF16)


def mem_attn(off, z, mk, mv, k_col, v_col, nb, t_len, tq, kv_t=False, bps=1):
    nq = t_len // tq
    w = MEM_TOKENS
    grid_spec = pltpu.PrefetchScalarGridSpec(
        num_scalar_prefetch=1,
        grid=(nb // bps, nq),
        in_specs=[pl.BlockSpec((bps * tq, w), lambda b, i, off: (b * nq + i, COL_MQ)),
                  pl.BlockSpec((bps * MEM_TOKENS, w), lambda b, i, off: (off[0] + b, k_col)),
                  pl.BlockSpec((bps * MEM_TOKENS, w), lambda b, i, off: (off[0] + b, v_col))],
        out_specs=pl.BlockSpec((bps * tq, w), lambda b, i, off: (b * nq + i, 0)),
    )
    return pl.pallas_call(
        functools.partial(_mem_attn_kernel, kv_t=kv_t, bps=bps),
        grid_spec=grid_spec,
        out_shape=SDS((nb * t_len, w), BF16),
        compiler_params=_cparams(2),
        name="mem_attn",
    )(off, z, mk, mv)


def _merge_kernel(h_ref, ys_ref, oa_ref, oh_ref, om_ref, g0, g1, g2, g3, ws, wa, wh, wm, wo, o_ref):
    m = (g0[...] * _dot(ys_ref[...], ws[...]) + g1[...] * _dot(oa_ref[...], wa[...])
         + g2[...] * _dot(oh_ref[...], wh[...]) + g3[...] * _dot(om_ref[...], wm[...]))
    o_ref[...] = h_ref[...] + _dot(m.astype(BF16), wo[...])


def merge_out(h, ys, oa, oh, om, z, ws, wa, wh, wm, wo, tm):
    n = h.shape[0]
    tm = min(tm, n)
    rows = lambda a: pl.BlockSpec((tm, a.shape[1]), lambda i: (i, 0))
    full = lambda a: pl.BlockSpec(a.shape, lambda i: (0, 0))
    gate = lambda k: pl.BlockSpec((tm, D_MODEL), lambda i: (i, GATE_COL0 + k))
    return pl.pallas_call(
        _merge_kernel,
        grid=(n // tm,),
        in_specs=[rows(h), rows(ys), rows(oa), rows(oh), rows(om),
                  gate(0), gate(1), gate(2), gate(3),
                  full(ws), full(wa), full(wh), full(wm), full(wo)],
        out_specs=rows(h),
        out_shape=SDS(h.shape, F32),
        compiler_params=_cparams(1),
        name="merge_out",
    )(h, ys, oa, oh, om, z, z, z, z, ws, wa, wh, wm, wo)


def _ffn_kernel(h_ref, g_ref, wg_ref, wu_ref, wo_ref, gf_ref, o_ref, hn_sc, acc_sc, *, final):
    j = pl.program_id(1)
    nj = pl.num_programs(1)

    @pl.when(j == 0)
    def _():
        hn_sc[...] = _rms(h_ref[...], g_ref[...]).astype(BF16)
        acc_sc[...] = jnp.zeros(acc_sc.shape, F32)

    hn = hn_sc[...]
    act = _silu(_dot(hn, wg_ref[...])) * _dot(hn, wu_ref[...])
    acc_sc[...] += _dot(act.astype(BF16), wo_ref[...])

    @pl.when(j == nj - 1)
    def _():
        out = h_ref[...] + acc_sc[...]
        o_ref[...] = _rms(out, gf_ref[...]) if final else out


def ffn(h, g, w_in, w_out, gf, tm, tf, final):
    n = h.shape[0]
    tm = min(tm, n)
    nf = D_FF // tf
    return pl.pallas_call(
        functools.partial(_ffn_kernel, final=final),
        grid=(n // tm, nf),
        in_specs=[pl.BlockSpec((tm, D_MODEL), lambda i, j: (i, 0)),
                  pl.BlockSpec((1, D_MODEL), lambda i, j: (0, 0)),
                  pl.BlockSpec((D_MODEL, tf), lambda i, j: (0, j)),
                  pl.BlockSpec((D_MODEL, tf), lambda i, j: (0, nf + j)),
                  pl.BlockSpec((tf, D_MODEL), lambda i, j: (j, 0)),
                  pl.BlockSpec((1, D_MODEL), lambda i, j: (0, 0))],
        out_specs=pl.BlockSpec((tm, D_MODEL), lambda i, j: (i, 0)),
        out_shape=SDS(h.shape, F32),
        scratch_shapes=[pltpu.VMEM((tm, D_MODEL), BF16), pltpu.VMEM((tm, D_MODEL), F32)],
        compiler_params=_cparams(2),
        name="ffn",
    )(h, g, w_in, w_in, w_out, gf)


def _rope_tables(pos):
    inv = ROPE_THETA ** (-jnp.arange(0, 64, 2, dtype=F32) / 64)
    ang = pos[:, None] * inv[None, :]
    ang = jnp.concatenate([ang, ang, ang, ang], axis=-1)
    cos, sin = jnp.cos(ang), jnp.sin(ang)
    upper = (jnp.arange(HEAD_LANES) % 64) >= 32
    return cos, jnp.where(upper, sin, 0.0), jnp.where(upper, 0.0, -sin)


def _block_diag(a):
    g, r, c = a.shape
    eye = jnp.eye(g, dtype=a.dtype)
    return (a[:, :, None, :] * eye[:, None, :, None]).reshape(g * r, g * c)


def _s5_params(lam_re, lam_im, log_dt, b_re, b_im, c_re, c_im, d):
    dt = jnp.exp(log_dt.astype(F32))[:, None]
    lr, li = lam_re.astype(F32), lam_im.astype(F32)
    mag = jnp.exp(lr * dt)
    ar, ai = mag * jnp.cos(li * dt), mag * jnp.sin(li * dt)
    nr, ni = ar - 1.0, ai
    den = lr * lr + li * li
    cr, ci = (nr * lr + ni * li) / den, (ni * lr - nr * li) / den
    br, bi = b_re.astype(F32), b_im.astype(F32)
    bbr = cr[..., None] * br - ci[..., None] * bi
    bbi = cr[..., None] * bi + ci[..., None] * br
    ar, ai = ar.reshape(1, -1), ai.reshape(1, -1)
    pows_r, pows_i = [ar], [ai]
    for _ in range(CHUNK - 1):
        pr, pi = pows_r[-1], pows_i[-1]
        pows_r.append(pr * ar - pi * ai)
        pows_i.append(pr * ai + pi * ar)
    rep = lambda x: jnp.broadcast_to(x, (CHUNK, x.shape[-1]))
    pw = jnp.stack([rep(pows_r[0]), rep(pows_i[0]), rep(pows_r[1]), rep(pows_i[1]),
                    rep(pows_r[3]), rep(pows_i[3]),
                    jnp.concatenate(pows_r, axis=0), jnp.concatenate(pows_i, axis=0)])
    return {
        "bre": _block_diag(jnp.swapaxes(bbr, 1, 2)).astype(BF16),
        "bim": _block_diag(jnp.swapaxes(bbi, 1, 2)).astype(BF16),
        "cre": _block_diag(jnp.swapaxes(c_re.astype(F32), 1, 2)).astype(BF16),
        "cim": _block_diag(jnp.swapaxes(c_im.astype(F32), 1, 2)).astype(BF16),
        "d": d.astype(F32).reshape(1, -1),
        "pw": pw,
    }


def _state_to_rows(s):
    b, h, dk, dv = s.shape
    return jnp.swapaxes(s, 2, 3).reshape(b, h * dv, dk)


def _rows_to_state(s, h):
    b, hv, dk = s.shape
    return jnp.swapaxes(s.reshape(b, h, hv // h, dk), 2, 3)


def kernel(x_prompt, x_sample, mem_prompt, cache_k, cache_v, page_table, cache_mem_k, cache_mem_v, state_ssm_re, state_ssm_im, state_hgrn, norm_mix, w_in, ssm_lam_re, ssm_lam_im, ssm_log_dt, ssm_b_re, ssm_b_im, ssm_c_re, ssm_c_im, ssm_d, ssm_w_glu, att_lq1, att_lk1, att_lq2, att_lk2, att_norm, hg_gamma, hg_norm, mem_norm, w_mem_kv, wb_ssm, wb_att, wb_hg, wb_mem, w_out, norm_ffn, w_ffn_in, w_ffn_out, norm_final):
    depth = w_in.shape[0]
    b, s, _ = x_prompt.shape
    db, ds, _ = x_sample.shape
    n_pages, page = page_table.shape[1], cache_k.shape[2]
    past = n_pages * page
    n_phys = cache_k.shape[1]
    g_ssm, p_ssm = state_ssm_re.shape[2], state_ssm_re.shape[3]
    hgw = hg_gamma.shape[1]
    memw = w_mem_kv.shape[2] // 2

    u_w = ssm_b_re.shape[1] * ssm_b_re.shape[3]
    split = IN_WIDTH - N_HEADS * D_MODEL
    w_in_p = jnp.concatenate([w_in[:, :, u_w:split], w_in[:, :, :u_w], w_in[:, :, split:]],
                             axis=-1).astype(BF16)
    bf = lambda a: a.astype(BF16)
    w_glu, w_mkv = bf(ssm_w_glu), bf(w_mem_kv)
    wbs, wba, wbh, wbm, wo = bf(wb_ssm), bf(wb_att), bf(wb_hg), bf(wb_mem), bf(w_out)
    wfi, wfo = bf(w_ffn_in), bf(w_ffn_out)
    gam = jax.nn.softmax(hg_gamma.astype(F32), axis=0)
    lb_all = jnp.cumsum(gam, axis=0) - gam[0]

    tabs_p = _rope_tables(jnp.arange(s, dtype=F32))
    tabs_s = tuple(jnp.tile(t, (db, 1)) for t in _rope_tables(past + jnp.arange(ds, dtype=F32)))

    ck = cache_k.reshape(depth, n_phys, page * N_HEADS, HEAD_LANES)
    cv = cache_v.reshape(depth, n_phys, page * N_HEADS, HEAD_LANES)
    cmk = jnp.transpose(cache_mem_k, (0, 1, 3, 4, 2)).reshape(depth * db * memw, MEM_TOKENS)
    cmv = jnp.transpose(cache_mem_v, (0, 1, 3, 4, 2)).reshape(depth * db * memw, MEM_TOKENS)
    bps = math.gcd(db, 8)
    memp = mem_prompt.reshape(b * MEM_TOKENS, D_MODEL)

    hp = x_prompt.reshape(b * s, D_MODEL)
    hs = x_sample.reshape(db * ds, D_MODEL)
    zeros_p = jnp.zeros((b, 1, g_ssm * p_ssm), F32)
    zeros_hg = jnp.zeros((b, hgw, hgw // N_HEADS), F32)

    tm_p = min(1024, s)
    tm_s = min(1024, db * ds)
    tt_p = min(512, s)
    tq_p = min(512, s)

    outs = {k: [] for k in ("kp", "vp", "mkp", "mvp", "srp", "sip", "hgp", "ks", "vs", "srs", "sis", "hgs")}
    for l in range(depth):
        lam_init = 0.8 - 0.6 * math.exp(-0.3 * l)
        cst = jnp.full((1,), lam_init, F32)
        lyr = jnp.full((1,), l, jnp.int32)
        row = lambda a: a[l].astype(F32).reshape(1, -1)
        s5p = _s5_params(ssm_lam_re[l], ssm_lam_im[l], ssm_log_dt[l], ssm_b_re[l], ssm_b_im[l],
                         ssm_c_re[l], ssm_c_im[l], ssm_d[l])
        lq1, lk1, lq2, lk2 = row(att_lq1), row(att_lk1), row(att_lq2), row(att_lk2)
        gn_att = row(att_norm)
        gn_hg = jnp.tile(row(hg_norm), (1, N_HEADS))
        lb = lb_all[l].reshape(1, -1)
        final = l == depth - 1
        gfin = norm_final.astype(F32).reshape(1, -1)

        mkv = mem_kv_proj(memp, row(mem_norm), w_mkv[l])
        z, zb, k_rows, v_rows, vt = in_proj(hp, row(norm_mix), w_in_p[l], *tabs_p, tm_p, t_len=s)
        ys, sr, si = s5_mixer(z, zeros_p, zeros_p, s5p, w_glu[l], b, s, tt_p)
        oa = attn_prompt(cst, zb, vt.reshape(-1, s), lq1, lk1, lq2, lk2, gn_att, b, s, tq_p)
        oh, sh = hgrn_mixer(z, lb, gn_hg, zeros_hg, b, s, tt_p)
        om = mem_attn(jnp.zeros((1,), jnp.int32), z, mkv, mkv, 0, 1, b, s, tt_p)
        hp = merge_out(hp, ys, oa, oh, om, z, wbs[l], wba[l], wbh[l], wbm[l], wo[l], 256)
        hp = ffn(hp, row(norm_ffn), wfi[l], wfo[l], gfin, 512, D_FF // 2, final)
        outs["kp"].append(k_rows.reshape(b, s, N_HEADS, HEAD_LANES))
        outs["vp"].append(v_rows.reshape(b, s, N_HEADS, HEAD_LANES))
        outs["mkp"].append(mkv[:, :memw].reshape(b, MEM_TOKENS, N_HEADS, memw // N_HEADS))
        outs["mvp"].append(mkv[:, memw:].reshape(b, MEM_TOKENS, N_HEADS, memw // N_HEADS))
        outs["srp"].append(sr.reshape(b, g_ssm, p_ssm))
        outs["sip"].append(si.reshape(b, g_ssm, p_ssm))
        outs["hgp"].append(_rows_to_state(sh, N_HEADS))

        z, zb, k_rows, v_rows = in_proj(hs, row(norm_mix), w_in_p[l], *tabs_s, tm_s)
        ys, sr, si = s5_mixer(z, state_ssm_re[l].reshape(db, 1, -1), state_ssm_im[l].reshape(db, 1, -1),
                              s5p, w_glu[l], db, ds, ds)
        oa = attn_sample(page_table, lyr, cst, zb, ck, cv, lq1, lk1, lq2, lk2, gn_att, ds)
        oh, sh = hgrn_mixer(z, lb, gn_hg, _state_to_rows(state_hgrn[l]), db, ds, ds)
        om = mem_attn(jnp.full((1,), l * db // bps, jnp.int32), z, cmk, cmv, 0, 0, db, ds, ds,
                      kv_t=True, bps=bps)
        hs = merge_out(hs, ys, oa, oh, om, z, wbs[l], wba[l], wbh[l], wbm[l], wo[l], 256)
        hs = ffn(hs, row(norm_ffn), wfi[l], wfo[l], gfin, 512, D_FF // 2, final)
        outs["ks"].append(k_rows.reshape(db, ds, N_HEADS, HEAD_LANES))
        outs["vs"].append(v_rows.reshape(db, ds, N_HEADS, HEAD_LANES))
        outs["srs"].append(sr.reshape(db, g_ssm, p_ssm))
        outs["sis"].append(si.reshape(db, g_ssm, p_ssm))
        outs["hgs"].append(_rows_to_state(sh, N_HEADS))

    st = jnp.stack
    return (hp.reshape(b, s, D_MODEL), hs.reshape(db, ds, D_MODEL),
            st(outs["kp"]), st(outs["vp"]), st(outs["mkp"]), st(outs["mvp"]),
            st(outs["srp"]), st(outs["sip"]), st(outs["hgp"]),
            st(outs["ks"]), st(outs["vs"]), st(outs["srs"]), st(outs["sis"]), st(outs["hgs"]))
```

```python
import functools
import math

import jax
import jax.numpy as jnp
from jax import lax
from jax.experimental import pallas as pl
from jax.experimental.pallas import tpu as pltpu

F32 = jnp.float32
BF16 = jnp.bfloat16
SDS = jax.ShapeDtypeStruct

D_MODEL = 1024
EPS = 1e-6
ROPE_THETA = 10000.0
MASK_VALUE = -1e30
HG_F_FLOOR = 1e-20
N_HEADS = 4
HEAD_LANES = 128
MEM_TOKENS = 256
D_FF = 2816
IN_WIDTH = 7168
IN_TILE = 512
CHUNK = 8
HG_CHUNK = 16
LOG2E = math.log2(math.e)
Q_SCALE = 64 ** -0.5 * math.log2(math.e)
VT_ROWS = HEAD_LANES + 16
VMEM_LIMIT = 56 * 1024 * 1024

Z_TILE0, Z_TILES = 3, 3
GATE_TILE0 = 6
COL_HQ, COL_HF, COL_HI, COL_HG, COL_MQ, COL_U = 0, 1, 2, 3, 4, 5


def _cparams(n_axes):
    return pltpu.CompilerParams(dimension_semantics=("arbitrary",) * n_axes,
                                vmem_limit_bytes=VMEM_LIMIT)


def _sigmoid(x):
    return 0.5 * jnp.tanh(0.5 * x) + 0.5


def _silu(x):
    return x * _sigmoid(x)


def _rms(x, g):
    return x * lax.rsqrt(jnp.mean(x * x, axis=-1, keepdims=True) + EPS) * g


def _dot(a, b):
    return jnp.dot(a, b, preferred_element_type=F32)


def _dot_nt(a, b):
    return lax.dot_general(a, b, (((1,), (1,)), ((), ())), preferred_element_type=F32)


def _block_ones(n, blk, dtype):
    r = lax.broadcasted_iota(jnp.int32, (n, n), 0) // blk
    c = lax.broadcasted_iota(jnp.int32, (n, n), 1) // blk
    return r == c


def _inproj_kernel(x_ref, g_ref, w_ref, cos_ref, sa_ref, sb_ref, z_ref, zb_ref, zg_ref, kr_ref, vr_ref,
                   *rest):
    xn_ref, vt_ref = rest[-1], (rest[0] if len(rest) == 2 else None)
    j = pl.program_id(1)
    tm = x_ref.shape[0]
    half = IN_TILE // 2

    @pl.when(j == 0)
    def _():
        xn_ref[...] = _rms(x_ref[...], g_ref[...]).astype(BF16)

    def halves():
        for c in range(2):
            yield c, slice(c * half, (c + 1) * half), _dot(xn_ref[...], w_ref[:, c * half:(c + 1) * half])

    def rotary(scale, rows_ref):
        cos, sa, sb = cos_ref[...], sa_ref[...], sb_ref[...]
        for c, _, acc in halves():
            for hh in range(2 * c, 2 * c + 2):
                sl = slice(hh * HEAD_LANES, (hh + 1) * HEAD_LANES)
                x = acc[:, (hh - 2 * c) * HEAD_LANES:(hh - 2 * c + 1) * HEAD_LANES]
                r = (x * cos + pltpu.roll(x, 32, 1) * sa + pltpu.roll(x, 96, 1) * sb)
                zb_ref[:, sl] = (r * scale).astype(BF16) if scale is not None else r.astype(BF16)
                if rows_ref is not None:
                    rows_ref[pl.ds(hh, tm, stride=N_HEADS), :] = r

    @pl.when(j == 0)
    def _():
        rotary(Q_SCALE, None)

    @pl.when(j == 1)
    def _():
        rotary(None, kr_ref)

    @pl.when(j == 2)
    def _():
        for c, sl, acc in halves():
            zb_ref[:, sl] = acc.astype(BF16)
            for hh in range(2 * c, 2 * c + 2):
                vh = acc[:, (hh - 2 * c) * HEAD_LANES:(hh - 2 * c + 1) * HEAD_LANES]
                vr_ref[pl.ds(hh, tm, stride=N_HEADS), :] = vh
                if vt_ref is not None:
                    extra = lax.broadcasted_iota(jnp.int32, (VT_ROWS - HEAD_LANES, tm), 0) == 0
                    vt_ref[hh, 0:HEAD_LANES, :] = vh.T.astype(BF16)
                    vt_ref[hh, HEAD_LANES:, :] = jnp.where(extra, 1.0, 0.0).astype(BF16)

    @pl.when(jnp.logical_and(j > 2, j < 6))
    def _():
        for _, sl, acc in halves():
            z_ref[:, sl] = acc

    @pl.when(j >= 6)
    def _():
        for _, sl, acc in halves():
            zg_ref[:, sl] = _sigmoid(acc).astype(BF16)


def in_proj(h, g, w, cos_t, sa_t, sb_t, tm, t_len=None):
    n = h.shape[0]
    nper = cos_t.shape[0] // tm
    tab = pl.BlockSpec((tm, HEAD_LANES), lambda i, j: (i % nper, 0))
    rows = pl.BlockSpec((tm * N_HEADS, HEAD_LANES), lambda i, j: (i, 0))
    vt_specs, vt_shapes = [], []
    if t_len is not None:
        per_seq = t_len // tm
        vt_specs = [pl.BlockSpec((N_HEADS, VT_ROWS, tm), lambda i, j: (i // per_seq, 0, i % per_seq))]
        vt_shapes = [SDS((n // t_len * N_HEADS, VT_ROWS, t_len), BF16)]
    return pl.pallas_call(
        _inproj_kernel,
        grid=(n // tm, IN_WIDTH // IN_TILE),
        in_specs=[pl.BlockSpec((tm, D_MODEL), lambda i, j: (i, 0)),
                  pl.BlockSpec((1, D_MODEL), lambda i, j: (0, 0)),
                  pl.BlockSpec((D_MODEL, IN_TILE), lambda i, j: (0, j)),
                  tab, tab, tab],
        out_specs=[pl.BlockSpec((tm, IN_TILE), lambda i, j: (i, jnp.clip(j - Z_TILE0, 0, Z_TILES - 1))),
                   pl.BlockSpec((tm, IN_TILE), lambda i, j: (i, jnp.minimum(j, 2))),
                   pl.BlockSpec((tm, IN_TILE), lambda i, j: (i, jnp.maximum(j - GATE_TILE0, 0))),
                   rows, rows] + vt_specs,
        out_shape=[SDS((n, Z_TILES * IN_TILE), F32), SDS((n, 3 * IN_TILE), BF16),
                   SDS((n, IN_WIDTH - GATE_TILE0 * IN_TILE), BF16),
                   SDS((n * N_HEADS, HEAD_LANES), F32), SDS((n * N_HEADS, HEAD_LANES), F32)] + vt_shapes,
        scratch_shapes=[pltpu.VMEM((tm, D_MODEL), BF16)],
        compiler_params=_cparams(2),
        name="in_proj",
    )(h, g, w, cos_t, sa_t, sb_t)


def _memkv_kernel(x_ref, g_ref, w_ref, o_ref):
    o_ref[...] = _dot(_rms(x_ref[...], g_ref[...]).astype(BF16), w_ref[...])


def mem_kv_proj(x, g, w):
    n, wd = x.shape[0], w.shape[1]
    tm = 256
    return pl.pallas_call(
        _memkv_kernel,
        grid=(n // tm,),
        in_specs=[pl.BlockSpec((tm, D_MODEL), lambda i: (i, 0)),
                  pl.BlockSpec((1, D_MODEL), lambda i: (0, 0)),
                  pl.BlockSpec((D_MODEL, wd), lambda i: (0, 0))],
        out_specs=pl.BlockSpec((tm, wd), lambda i: (i, 0)),
        out_shape=SDS((n, wd), F32),
        compiler_params=_cparams(1),
        name="mem_kv_proj",
    )(x, g, w)


def _s5_kernel(u_ref, h0r_ref, h0i_ref, bre_ref, bim_ref, cre_ref, cim_ref, d_ref, wg_ref,
               pw_ref, ys_ref, sr_ref, si_ref, xr_sc, xi_sc, hr_sc, hi_sc, *, per_seq):
    t = pl.program_id(1)
    tt = u_ref.shape[0]
    ns = xr_sc.shape[1]
    n_chunks = tt // CHUNK
    u = u_ref[...]
    ub = u.astype(BF16)
    xr_sc[...] = _dot(ub, bre_ref[...])
    xi_sc[...] = _dot(ub, bim_ref[...])
    row = lax.broadcasted_iota(jnp.int32, (CHUNK, 1), 0)
    apr, api = pw_ref[6], pw_ref[7]

    def scan_chunk(c, hr, hi):
        sl = pl.ds(pl.multiple_of(c * CHUNK, CHUNK), CHUNK)
        xr, xi = xr_sc[sl, :], xi_sc[sl, :]
        for k, s in enumerate((1, 2, 4)):
            ar, ai = pw_ref[2 * k], pw_ref[2 * k + 1]
            keep = row >= s
            pr = jnp.where(keep, pltpu.roll(xr, s, 0), 0.0)
            pi = jnp.where(keep, pltpu.roll(xi, s, 0), 0.0)
            xr, xi = xr + (ar * pr - ai * pi), xi + (ar * pi + ai * pr)
        xr, xi = xr + (apr * hr - api * hi), xi + (apr * hi + api * hr)
        xr_sc[sl, :] = xr
        xi_sc[sl, :] = xi
        return xr, xi

    if per_seq:
        def body(c, carry):
            one = pl.ds(c, 1)
            xr, xi = scan_chunk(c, jnp.broadcast_to(h0r_ref[one, :], (CHUNK, ns)),
                                jnp.broadcast_to(h0i_ref[one, :], (CHUNK, ns)))
            sr_ref[one, :] = xr[CHUNK - 1:CHUNK, :]
            si_ref[one, :] = xi[CHUNK - 1:CHUNK, :]
            return carry

        lax.fori_loop(0, n_chunks, body, 0, unroll=math.gcd(4, n_chunks))
    else:
        @pl.when(t == 0)
        def _():
            hr_sc[...] = jnp.broadcast_to(h0r_ref[...], hr_sc.shape)
            hi_sc[...] = jnp.broadcast_to(h0i_ref[...], hi_sc.shape)

        def body(c, carry):
            xr, xi = scan_chunk(c, *carry)
            return (jnp.broadcast_to(xr[CHUNK - 1:CHUNK, :], xr.shape),
                    jnp.broadcast_to(xi[CHUNK - 1:CHUNK, :], xi.shape))

        hr, hi = lax.fori_loop(0, n_chunks, body, (hr_sc[...], hi_sc[...]),
                               unroll=math.gcd(4, n_chunks))
        hr_sc[...] = hr
        hi_sc[...] = hi
        sr_ref[...] = hr[0:1, :]
        si_ref[...] = hi[0:1, :]

    y = (_dot(xr_sc[...].astype(BF16), cre_ref[...]) - _dot(xi_sc[...].astype(BF16), cim_ref[...])
         + d_ref[...] * u)
    ys = jax.nn.gelu(y)
    gate = _sigmoid(_dot(ys.astype(BF16), wg_ref[...]))
    ys_ref[...] = (ys * gate).astype(BF16)


def s5_mixer(z, h0r, h0i, prm, wglu, nb, t_len, tt):
    per_seq = t_len == CHUNK
    ns = h0r.shape[-1]
    width = prm["d"].shape[-1]
    full = lambda a: pl.BlockSpec(a.shape, lambda b, t: (0,) * a.ndim)
    if per_seq:
        grid, tt = (1, 1), nb * t_len
        h0r, h0i = h0r.reshape(nb, ns), h0i.reshape(nb, ns)
        st = pl.BlockSpec((nb, ns), lambda b, t: (0, 0))
        rows = lambda cidx: pl.BlockSpec((tt, width), lambda b, t: (0, cidx))
    else:
        nt = t_len // tt
        grid = (nb, nt)
        st = pl.BlockSpec((None, 1, ns), lambda b, t: (b, 0, 0))
        rows = lambda cidx: pl.BlockSpec((tt, width), lambda b, t: (b * nt + t, cidx))
    return pl.pallas_call(
        functools.partial(_s5_kernel, per_seq=per_seq),
        grid=grid,
        in_specs=[rows(COL_U), st, st,
                  full(prm["bre"]), full(prm["bim"]), full(prm["cre"]), full(prm["cim"]),
                  full(prm["d"]), full(wglu), full(prm["pw"])],
        out_specs=[rows(0), st, st],
        out_shape=[SDS((nb * t_len, width), BF16), SDS(h0r.shape, F32), SDS(h0r.shape, F32)],
        scratch_shapes=[pltpu.VMEM((tt, ns), F32), pltpu.VMEM((tt, ns), F32),
                        pltpu.VMEM((CHUNK, ns), F32), pltpu.VMEM((CHUNK, ns), F32)],
        compiler_params=_cparams(2),
        name="s5",
    )(z, h0r, h0i, prm["bre"], prm["bim"], prm["cre"], prm["cim"], prm["d"], wglu, prm["pw"])


def _lambda(lq1, lk1, lq2, lk2, lam_init):
    return (jnp.exp(jnp.sum(lq1[...] * lk1[...], axis=-1, keepdims=True))
            - jnp.exp(jnp.sum(lq2[...] * lk2[...], axis=-1, keepdims=True)) + lam_init)


def _two_maps(q):
    lane = lax.broadcasted_iota(jnp.int32, (1, HEAD_LANES), 1)
    zero = jnp.zeros_like(q)
    return jnp.concatenate([jnp.where(lane < 64, q, zero), jnp.where(lane >= 64, q, zero)], axis=0)


def _diff_finish(o1, o2, lam, lam_init, gn):
    return _rms(o1 - lam * o2, gn) * (1.0 - lam_init)


def _attn_prompt_kernel(cst_ref, q_ref, k_ref, vt_ref, lq1, lk1, lq2, lk2, gn_ref, o_ref,
                        m_sc, acc_sc, s0_sc, s1_sc, b0_sc, b1_sc):
    qi = pl.program_id(2)
    tq = q_ref.shape[0]
    w = 2 * tq
    lam_init = cst_ref[0]
    qq = _two_maps(q_ref[...])
    m_sc[...] = jnp.full(m_sc.shape, MASK_VALUE, F32)
    acc_sc[...] = jnp.zeros(acc_sc.shape, F32)
    slots = ((s0_sc, b0_sc), (s1_sc, b1_sc))

    def scores(kb, slot):
        sl = pl.ds(pl.multiple_of(kb * tq, tq), tq)
        s = _dot_nt(k_ref[sl, :], qq)
        slots[slot][0][...] = s
        slots[slot][1][...] = jnp.max(s, axis=0, keepdims=True)

    def consume(kb, slot, masked):
        sl = pl.ds(pl.multiple_of(kb * tq, tq), tq)
        s = slots[slot][0][...]
        if masked:
            key = lax.broadcasted_iota(jnp.int32, (tq, w), 0)
            qry = lax.broadcasted_iota(jnp.int32, (tq, w), 1) % tq
            s = jnp.where(key <= qry, s, MASK_VALUE)
            bmax = jnp.max(s, axis=0, keepdims=True)
        else:
            bmax = slots[slot][1][...]
        m_prev = m_sc[...]
        m_new = jnp.maximum(m_prev, bmax)
        p = jnp.exp2(s - m_new)
        acc_sc[...] = jnp.exp2(m_prev - m_new) * acc_sc[...] + _dot(vt_ref[:, sl], p.astype(BF16))
        m_sc[...] = m_new

    def body(j, carry):
        scores(2 * j + 1, 1)
        consume(2 * j, 0, False)
        scores(2 * j + 2, 0)
        consume(2 * j + 1, 1, False)
        return carry

    scores(0, 0)
    lax.fori_loop(0, qi // 2, body, 0)

    @pl.when(qi % 2 == 0)
    def _():
        consume(qi, 0, True)

    @pl.when(qi % 2 == 1)
    def _():
        scores(qi, 1)
        consume(qi - 1, 0, False)
        consume(qi, 1, True)

    lam = _lambda(lq1, lk1, lq2, lk2, lam_init)
    acc = acc_sc[...]
    ot = acc[:HEAD_LANES, :] / acc[HEAD_LANES:HEAD_LANES + 1, :]
    o_ref[...] = _diff_finish(ot[:, :tq].T, ot[:, tq:].T, lam, lam_init, gn_ref[...]).astype(BF16)


def attn_prompt(cst, zb, vt, lq1, lk1, lq2, lk2, gn, nb, t_len, tq):
    nq = t_len // tq
    vec = pl.BlockSpec((1, 64), lambda b, h, i: (0, 0))
    return pl.pallas_call(
        _attn_prompt_kernel,
        grid=(nb, N_HEADS, nq),
        in_specs=[pl.BlockSpec(memory_space=pltpu.SMEM),
                  pl.BlockSpec((tq, HEAD_LANES), lambda b, h, i: (b * nq + i, h)),
                  pl.BlockSpec((t_len, HEAD_LANES), lambda b, h, i: (b, N_HEADS + h)),
                  pl.BlockSpec((VT_ROWS, t_len), lambda b, h, i: (b * N_HEADS + h, 0)),
                  vec, vec, vec, vec,
                  pl.BlockSpec((1, HEAD_LANES), lambda b, h, i: (0, 0))],
        out_specs=pl.BlockSpec((tq, HEAD_LANES), lambda b, h, i: (b * nq + i, h)),
        out_shape=SDS((nb * t_len, N_HEADS * HEAD_LANES), BF16),
        scratch_shapes=[pltpu.VMEM((1, 2 * tq), F32), pltpu.VMEM((VT_ROWS, 2 * tq), F32),
                        pltpu.VMEM((tq, 2 * tq), F32), pltpu.VMEM((tq, 2 * tq), F32),
                        pltpu.VMEM((1, 2 * tq), F32), pltpu.VMEM((1, 2 * tq), F32)],
        compiler_params=_cparams(3),
        name="attn_prompt",
    )(cst, zb, zb, vt, lq1, lk1, lq2, lk2, gn)


def _attn_sample_kernel(pt_ref, lyr_ref, cst_ref, q_ref, kn_ref, vn_ref, lq1, lk1, lq2, lk2, gn_ref,
                        *rest):
    n_pages = (len(rest) - 1) // 2
    k_refs, v_refs, o_ref = rest[:n_pages], rest[n_pages:2 * n_pages], rest[-1]
    t = q_ref.shape[0]
    lam_init = cst_ref[0]
    lam = _lambda(lq1, lk1, lq2, lk2, lam_init)
    hsl = [slice(hh * HEAD_LANES, (hh + 1) * HEAD_LANES) for hh in range(N_HEADS)]
    qh = [_two_maps(q_ref[:, sl]) for sl in hsl]
    qq = jnp.concatenate(qh, axis=0)
    rows, cols = N_HEADS * 2 * t, k_refs[0].shape[0]
    same_head = (lax.broadcasted_iota(jnp.int32, (rows, cols), 0) // (2 * t)
                 == lax.broadcasted_iota(jnp.int32, (rows, cols), 1) % N_HEADS)
    s_past = [jnp.where(same_head, _dot_nt(qq, k_refs[p][...].astype(BF16)), MASK_VALUE)
              for p in range(n_pages)]
    r = lax.broadcasted_iota(jnp.int32, (2 * t, t), 0) % t
    c = lax.broadcasted_iota(jnp.int32, (2 * t, t), 1)
    s_new = jnp.concatenate([jnp.where(c <= r, _dot_nt(qh[hh], kn_ref[:, hsl[hh]]), MASK_VALUE)
                             for hh in range(N_HEADS)], axis=0)
    m = s_past[0]
    for sp in s_past[1:]:
        m = jnp.maximum(m, sp)
    m = jnp.maximum(jnp.max(m, axis=-1, keepdims=True), jnp.max(s_new, axis=-1, keepdims=True))
    p_new = jnp.exp2(s_new - m)
    l = jnp.sum(p_new, axis=-1, keepdims=True)
    acc = jnp.concatenate([_dot(p_new[hh * 2 * t:(hh + 1) * 2 * t].astype(BF16), vn_ref[:, hsl[hh]])
                           for hh in range(N_HEADS)], axis=0)
    lsum = jnp.zeros(s_past[0].shape, F32)
    for p in range(n_pages):
        pp = jnp.exp2(s_past[p] - m)
        lsum = lsum + pp
        acc = acc + _dot(pp.astype(BF16), v_refs[p][...].astype(BF16))
    o = acc / (l + jnp.sum(lsum, axis=-1, keepdims=True))
    for hh in range(N_HEADS):
        o1 = o[hh * 2 * t:hh * 2 * t + t]
        o2 = o[hh * 2 * t + t:(hh + 1) * 2 * t]
        o_ref[:, hsl[hh]] = _diff_finish(o1, o2, lam, lam_init, gn_ref[...]).astype(BF16)


def attn_sample(page_table, lyr, cst, zb, cache_k, cache_v, lq1, lk1, lq2, lk2, gn, t_len):
    nb, n_pages = page_table.shape
    page_rows = cache_k.shape[2]
    width = N_HEADS * HEAD_LANES
    pt = page_table.reshape(-1)
    vec = pl.BlockSpec((1, 64), lambda b, pt, ly: (0, 0))

    def page_spec(p):
        return pl.BlockSpec((None, None, page_rows, HEAD_LANES),
                            lambda b, pt, ly: (ly[0], pt[b * n_pages + p], 0, 0))

    pages = [page_spec(p) for p in range(n_pages)]
    grid_spec = pltpu.PrefetchScalarGridSpec(
        num_scalar_prefetch=2,
        grid=(nb,),
        in_specs=[pl.BlockSpec(memory_space=pltpu.SMEM),
                  pl.BlockSpec((t_len, width), lambda b, pt, ly: (b, 0)),
                  pl.BlockSpec((t_len, width), lambda b, pt, ly: (b, 1)),
                  pl.BlockSpec((t_len, width), lambda b, pt, ly: (b, 2)),
                  vec, vec, vec, vec,
                  pl.BlockSpec((1, HEAD_LANES), lambda b, pt, ly: (0, 0))] + pages + pages,
        out_specs=pl.BlockSpec((t_len, width), lambda b, pt, ly: (b, 0)),
    )
    return pl.pallas_call(
        _attn_sample_kernel,
        grid_spec=grid_spec,
        out_shape=SDS((nb * t_len, width), BF16),
        compiler_params=_cparams(1),
        name="attn_sample",
    )(pt, lyr, cst, zb, zb, zb, lq1, lk1, lq2, lk2, gn,
      *([cache_k] * n_pages), *([cache_v] * n_pages))


def _hgrn_kernel(hq_ref, hf_ref, hi_ref, hg_ref, lb_ref, gn_ref, s0_ref, oh_ref, sout_ref,
                 s_sc, u_sc, hist_sc, dec_sc, o_sc, *, per_seq, chunk):
    t = pl.program_id(1)
    nt = pl.num_programs(1)
    tt, w = hq_ref.shape
    dk = w // N_HEADS
    n = tt // chunk
    blk = _block_ones(w, dk, F32)
    ones_bf = jnp.where(blk, 1.0, 0.0).astype(BF16)

    def expand(s0):
        return jnp.where(blk, jnp.concatenate([s0] * N_HEADS, axis=1), 0.0)

    def contract(sf):
        acc = sf[:, 0:dk]
        for hh in range(1, N_HEADS):
            acc = acc + sf[:, hh * dk:(hh + 1) * dk]
        return acc

    if not per_seq:
        @pl.when(t == 0)
        def _():
            s_sc[...] = expand(s0_ref[...])

    lb = lb_ref[...]
    f = lb + (1.0 - lb) / (1.0 + jnp.exp(-hf_ref[...]))
    k = 1.0 - f
    q = _silu(hq_ref[...])
    v = hi_ref[...]
    row = lax.broadcasted_iota(jnp.int32, (tt, 1), 0) % chunk
    cum = jnp.log(jnp.maximum(f, HG_F_FLOOR))
    s = 1
    while s < chunk:
        cum = cum + jnp.where(row >= s, pltpu.roll(cum, s, 0), 0.0)
        s *= 2
    cum2 = cum * LOG2E

    def chunk_row(x, s):
        x3 = x.reshape(n, chunk, w)
        return jnp.broadcast_to(x3[:, s:s + 1, :], (n, chunk, w)).reshape(tt, w)

    def tail(x, j):
        rows = chunk - j * CHUNK
        return x.reshape(n, chunk, x.shape[-1])[:, j * CHUNK:, :].reshape(n * rows, x.shape[-1])

    groups = chunk // CHUNK
    o_parts = [None] * groups
    for s in range(chunk):
        j = s // CHUNK
        keep = tail(row, j) >= s
        e = jnp.where(keep, jnp.exp2(jnp.minimum(tail(cum2, j) - tail(chunk_row(cum2, s), j), 0.0)), 0.0)
        att = _dot((tail(q, j) * e * tail(chunk_row(k, s), j)).astype(BF16), ones_bf)
        term = att * tail(chunk_row(v, s), j)
        o_parts[j] = term if o_parts[j] is None else o_parts[j] + term
    o3 = None
    for j in range(groups):
        pj = o_parts[j].reshape(n, chunk - j * CHUNK, w)
        if j:
            pj = jnp.concatenate([jnp.zeros((n, j * CHUNK, w), F32), pj], axis=1)
        o3 = pj if o3 is None else o3 + pj
    o_sc[...] = o3.reshape(tt, w)
    last = chunk_row(cum, chunk - 1)
    qd = (q * jnp.exp(cum)).astype(BF16)
    dec_sc[...] = jnp.exp(last)

    vb = v.astype(BF16)
    kb = (k * jnp.exp(last - cum)).astype(BF16)
    for c in range(n):
        sl = slice(c * chunk, (c + 1) * chunk)
        lhs, rhs = (vb[sl], kb[sl]) if chunk >= 16 else (vb[sl].astype(F32), kb[sl].astype(F32))
        upd = lax.dot_general(lhs, rhs, (((0,), (0,)), ((), ())), preferred_element_type=F32)
        u_sc[c] = jnp.where(blk, upd, 0.0)

    if per_seq:
        for c in range(n):
            s_old = expand(s0_ref[c])
            hist_sc[c] = s_old.astype(BF16)
            sout_ref[c] = contract(dec_sc[c * chunk:c * chunk + 1, :] * s_old + u_sc[c])
    else:
        def body(c, s_old):
            hist_sc[c] = s_old.astype(BF16)
            return dec_sc[pl.ds(pl.multiple_of(c * chunk, chunk), 1), :] * s_old + u_sc[c]

        s_sc[...] = lax.fori_loop(0, n, body, s_sc[...], unroll=math.gcd(4, n))

    for c in range(n):
        sl = slice(c * chunk, (c + 1) * chunk)
        o_sc[sl, :] = o_sc[sl, :] + _dot_nt(qd[sl], hist_sc[c])

    o = o_sc[...]
    o2 = o * o
    hi = o2.astype(BF16)
    lo = (o2 - hi.astype(F32)).astype(BF16)
    ms = (_dot(hi, ones_bf) + _dot(lo, ones_bf)) * (1.0 / dk)
    on = o * lax.rsqrt(ms + EPS) * gn_ref[...]
    oh_ref[...] = (on * _silu(hg_ref[...])).astype(BF16)

    if not per_seq:
        @pl.when(t == nt - 1)
        def _():
            sout_ref[...] = contract(s_sc[...])


def hgrn_mixer(z, lb, gn, s0, nb, t_len, tt):
    per_seq = t_len == CHUNK
    w = lb.shape[-1]
    dk = w // N_HEADS
    row = pl.BlockSpec((1, w), lambda b, t: (0, 0))
    if per_seq:
        sps = math.gcd(nb, 16)
        nt, tt, grid, chunk = 1, sps * t_len, (nb // sps, 1), t_len
        st = pl.BlockSpec((sps, w, dk), lambda b, t: (b, 0, 0))
    else:
        nt = t_len // tt
        grid = (nb, nt)
        chunk = math.gcd(HG_CHUNK, tt)
        st = pl.BlockSpec((None, w, dk), lambda b, t: (b, 0, 0))
    col = lambda cidx: pl.BlockSpec((tt, w), lambda b, t: (b * nt + t, cidx))
    return pl.pallas_call(
        functools.partial(_hgrn_kernel, per_seq=per_seq, chunk=chunk),
        grid=grid,
        in_specs=[col(COL_HQ), col(COL_HF), col(COL_HI), col(COL_HG), row, row, st],
        out_specs=[pl.BlockSpec((tt, w), lambda b, t: (b * nt + t, 0)), st],
        out_shape=[SDS((nb * t_len, w), BF16), SDS((nb, w, dk), F32)],
        scratch_shapes=[pltpu.VMEM((w, w), F32), pltpu.VMEM((tt // chunk, w, w), F32),
                        pltpu.VMEM((tt // chunk, w, w), BF16),
                        pltpu.VMEM((tt, w), F32), pltpu.VMEM((tt, w), F32)],
        compiler_params=_cparams(2),
        name="hgrn",
    )(z, z, z, z, lb, gn, s0)


def _mem_attn_kernel(off_ref, q_ref, mk_ref, mv_ref, o_ref, *, kv_t, bps):
    w = q_ref.shape[1]
    tq = q_ref.shape[0] // bps
    dh = w // N_HEADS
    head = lax.broadcasted_iota(jnp.int32, (1, w), 1) // dh
    for e in range(bps):
        q = q_ref[e * tq:(e + 1) * tq, :].astype(BF16) * jnp.asarray(dh ** -0.5, BF16)
        mk = mk_ref[e * w:(e + 1) * w, :].astype(BF16)
        mv = mv_ref[e * w:(e + 1) * w, :].astype(BF16)
        out = jnp.zeros((tq, w), F32)
        for hh in range(N_HEADS):
            qh = jnp.where(head == hh, q, jnp.zeros_like(q))
            s = _dot(qh, mk) if kv_t else _dot_nt(qh, mk)
            p = jnp.exp(s - jnp.max(s, axis=-1, keepdims=True))
            p = (p / jnp.sum(p, axis=-1, keepdims=True)).astype(BF16)
            out = out + jnp.where(head == hh, _dot_nt(p, mv) if kv_t else _dot(p, mv), 0.0)
        o_ref[e * tq:(e + 1) * tq, :] = out.astype(BF16)


def mem_attn(off, z, mk, mv, k_col, v_col, nb, t_len, tq, kv_t=False, bps=1):
    nq = t_len // tq
    w = MEM_TOKENS
    grid_spec = pltpu.PrefetchScalarGridSpec(
        num_scalar_prefetch=1,
        grid=(nb // bps, nq),
        in_specs=[pl.BlockSpec((bps * tq, w), lambda b, i, off: (b * nq + i, COL_MQ)),
                  pl.BlockSpec((bps * MEM_TOKENS, w), lambda b, i, off: (off[0] + b, k_col)),
                  pl.BlockSpec((bps * MEM_TOKENS, w), lambda b, i, off: (off[0] + b, v_col))],
        out_specs=pl.BlockSpec((bps * tq, w), lambda b, i, off: (b * nq + i, 0)),
    )
    return pl.pallas_call(
        functools.partial(_mem_attn_kernel, kv_t=kv_t, bps=bps),
        grid_spec=grid_spec,
        out_shape=SDS((nb * t_len, w), BF16),
        compiler_params=_cparams(2),
        name="mem_attn",
    )(off, z, mk, mv)


def _merge_kernel(h_ref, ys_ref, oa_ref, oh_ref, om_ref, g0, g1, g2, g3, ws, wa, wh, wm, wo, o_ref):
    m = (g0[...] * _dot(ys_ref[...], ws[...]) + g1[...] * _dot(oa_ref[...], wa[...])
         + g2[...] * _dot(oh_ref[...], wh[...]) + g3[...] * _dot(om_ref[...], wm[...]))
    o_ref[...] = h_ref[...] + _dot(m.astype(BF16), wo[...])


def merge_out(h, ys, oa, oh, om, z, ws, wa, wh, wm, wo, tm):
    n = h.shape[0]
    tm = min(tm, n)
    rows = lambda a: pl.BlockSpec((tm, a.shape[1]), lambda i: (i, 0))
    full = lambda a: pl.BlockSpec(a.shape, lambda i: (0, 0))
    gate = lambda k: pl.BlockSpec((tm, D_MODEL), lambda i: (i, k))
    return pl.pallas_call(
        _merge_kernel,
        grid=(n // tm,),
        in_specs=[rows(h), rows(ys), rows(oa), rows(oh), rows(om),
                  gate(0), gate(1), gate(2), gate(3),
                  full(ws), full(wa), full(wh), full(wm), full(wo)],
        out_specs=rows(h),
        out_shape=SDS(h.shape, F32),
        compiler_params=_cparams(1),
        name="merge_out",
    )(h, ys, oa, oh, om, z, z, z, z, ws, wa, wh, wm, wo)


def _ffn_kernel(h_ref, g_ref, wg_ref, wu_ref, wo_ref, gf_ref, o_ref, hn_sc, acc_sc, *, final):
    j = pl.program_id(1)
    nj = pl.num_programs(1)

    @pl.when(j == 0)
    def _():
        hn_sc[...] = _rms(h_ref[...], g_ref[...]).astype(BF16)
        acc_sc[...] = jnp.zeros(acc_sc.shape, F32)

    hn = hn_sc[...]
    act = _silu(_dot(hn, wg_ref[...])) * _dot(hn, wu_ref[...])
    acc_sc[...] += _dot(act.astype(BF16), wo_ref[...])

    @pl.when(j == nj - 1)
    def _():
        out = h_ref[...] + acc_sc[...]
        o_ref[...] = _rms(out, gf_ref[...]) if final else out


def ffn(h, g, w_in, w_out, gf, tm, tf, final):
    n = h.shape[0]
    tm = min(tm, n)
    nf = D_FF // tf
    return pl.pallas_call(
        functools.partial(_ffn_kernel, final=final),
        grid=(n // tm, nf),
        in_specs=[pl.BlockSpec((tm, D_MODEL), lambda i, j: (i, 0)),
                  pl.BlockSpec((1, D_MODEL), lambda i, j: (0, 0)),
                  pl.BlockSpec((D_MODEL, tf), lambda i, j: (0, j)),
                  pl.BlockSpec((D_MODEL, tf), lambda i, j: (0, nf + j)),
                  pl.BlockSpec((tf, D_MODEL), lambda i, j: (j, 0)),
                  pl.BlockSpec((1, D_MODEL), lambda i, j: (0, 0))],
        out_specs=pl.BlockSpec((tm, D_MODEL), lambda i, j: (i, 0)),
        out_shape=SDS(h.shape, F32),
        scratch_shapes=[pltpu.VMEM((tm, D_MODEL), BF16), pltpu.VMEM((tm, D_MODEL), F32)],
        compiler_params=_cparams(2),
        name="ffn",
    )(h, g, w_in, w_in, w_out, gf)


def _rope_tables(pos):
    inv = ROPE_THETA ** (-jnp.arange(0, 64, 2, dtype=F32) / 64)
    ang = pos[:, None] * inv[None, :]
    ang = jnp.concatenate([ang, ang, ang, ang], axis=-1)
    cos, sin = jnp.cos(ang), jnp.sin(ang)
    upper = (jnp.arange(HEAD_LANES) % 64) >= 32
    return cos, jnp.where(upper, sin, 0.0), jnp.where(upper, 0.0, -sin)


def _block_diag(a):
    g, r, c = a.shape
    eye = jnp.eye(g, dtype=a.dtype)
    return (a[:, :, None, :] * eye[:, None, :, None]).reshape(g * r, g * c)


def _s5_params(lam_re, lam_im, log_dt, b_re, b_im, c_re, c_im, d):
    dt = jnp.exp(log_dt.astype(F32))[:, None]
    lr, li = lam_re.astype(F32), lam_im.astype(F32)
    mag = jnp.exp(lr * dt)
    ar, ai = mag * jnp.cos(li * dt), mag * jnp.sin(li * dt)
    nr, ni = ar - 1.0, ai
    den = lr * lr + li * li
    cr, ci = (nr * lr + ni * li) / den, (ni * lr - nr * li) / den
    br, bi = b_re.astype(F32), b_im.astype(F32)
    bbr = cr[..., None] * br - ci[..., None] * bi
    bbi = cr[..., None] * bi + ci[..., None] * br
    ar, ai = ar.reshape(1, -1), ai.reshape(1, -1)
    pows_r, pows_i = [ar], [ai]
    for _ in range(CHUNK - 1):
        pr, pi = pows_r[-1], pows_i[-1]
        pows_r.append(pr * ar - pi * ai)
        pows_i.append(pr * ai + pi * ar)
    rep = lambda x: jnp.broadcast_to(x, (CHUNK, x.shape[-1]))
    pw = jnp.stack([rep(pows_r[0]), rep(pows_i[0]), rep(pows_r[1]), rep(pows_i[1]),
                    rep(pows_r[3]), rep(pows_i[3]),
                    jnp.concatenate(pows_r, axis=0), jnp.concatenate(pows_i, axis=0)])
    return {
        "bre": _block_diag(jnp.swapaxes(bbr, 1, 2)).astype(BF16),
        "bim": _block_diag(jnp.swapaxes(bbi, 1, 2)).astype(BF16),
        "cre": _block_diag(jnp.swapaxes(c_re.astype(F32), 1, 2)).astype(BF16),
        "cim": _block_diag(jnp.swapaxes(c_im.astype(F32), 1, 2)).astype(BF16),
        "d": d.astype(F32).reshape(1, -1),
        "pw": pw,
    }


def _state_to_rows(s):
    b, h, dk, dv = s.shape
    return jnp.swapaxes(s, 2, 3).reshape(b, h * dv, dk)


def _rows_to_state(s, h):
    b, hv, dk = s.shape
    return jnp.swapaxes(s.reshape(b, h, hv // h, dk), 2, 3)


def kernel(x_prompt, x_sample, mem_prompt, cache_k, cache_v, page_table, cache_mem_k, cache_mem_v, state_ssm_re, state_ssm_im, state_hgrn, norm_mix, w_in, ssm_lam_re, ssm_lam_im, ssm_log_dt, ssm_b_re, ssm_b_im, ssm_c_re, ssm_c_im, ssm_d, ssm_w_glu, att_lq1, att_lk1, att_lq2, att_lk2, att_norm, hg_gamma, hg_norm, mem_norm, w_mem_kv, wb_ssm, wb_att, wb_hg, wb_mem, w_out, norm_ffn, w_ffn_in, w_ffn_out, norm_final):
    depth = w_in.shape[0]
    b, s, _ = x_prompt.shape
    db, ds, _ = x_sample.shape
    n_pages, page = page_table.shape[1], cache_k.shape[2]
    past = n_pages * page
    n_phys = cache_k.shape[1]
    g_ssm, p_ssm = state_ssm_re.shape[2], state_ssm_re.shape[3]
    hgw = hg_gamma.shape[1]
    memw = w_mem_kv.shape[2] // 2

    u_w = ssm_b_re.shape[1] * ssm_b_re.shape[3]
    split = IN_WIDTH - N_HEADS * D_MODEL
    w_in_p = jnp.concatenate([w_in[:, :, u_w:split], w_in[:, :, :u_w], w_in[:, :, split:]],
                             axis=-1).astype(BF16)
    bf = lambda a: a.astype(BF16)
    w_glu, w_mkv = bf(ssm_w_glu), bf(w_mem_kv)
    wbs, wba, wbh, wbm, wo = bf(wb_ssm), bf(wb_att), bf(wb_hg), bf(wb_mem), bf(w_out)
    wfi, wfo = bf(w_ffn_in), bf(w_ffn_out)
    gam = jax.nn.softmax(hg_gamma.astype(F32), axis=0)
    lb_all = jnp.cumsum(gam, axis=0) - gam[0]

    tabs_p = _rope_tables(jnp.arange(s, dtype=F32))
    tabs_s = tuple(jnp.tile(t, (db, 1)) for t in _rope_tables(past + jnp.arange(ds, dtype=F32)))

    ck = cache_k.reshape(depth, n_phys, page * N_HEADS, HEAD_LANES)
    cv = cache_v.reshape(depth, n_phys, page * N_HEADS, HEAD_LANES)
    cmk = jnp.transpose(cache_mem_k, (0, 1, 3, 4, 2)).reshape(depth * db * memw, MEM_TOKENS)
    cmv = jnp.transpose(cache_mem_v, (0, 1, 3, 4, 2)).reshape(depth * db * memw, MEM_TOKENS)
    bps = math.gcd(db, 8)
    memp = mem_prompt.reshape(b * MEM_TOKENS, D_MODEL)

    hp = x_prompt.reshape(b * s, D_MODEL)
    hs = x_sample.reshape(db * ds, D_MODEL)
    zeros_p = jnp.zeros((b, 1, g_ssm * p_ssm), F32)
    zeros_hg = jnp.zeros((b, hgw, hgw // N_HEADS), F32)

    tm_p = min(1024, s)
    tm_s = min(1024, db * ds)
    tt_p = min(512, s)
    tq_p = min(512, s)

    outs = {k: [] for k in ("kp", "vp", "mkp", "mvp", "srp", "sip", "hgp", "ks", "vs", "srs", "sis", "hgs")}
    for l in range(depth):
        lam_init = 0.8 - 0.6 * math.exp(-0.3 * l)
        cst = jnp.full((1,), lam_init, F32)
        lyr = jnp.full((1,), l, jnp.int32)
        row = lambda a: a[l].astype(F32).reshape(1, -1)
        s5p = _s5_params(ssm_lam_re[l], ssm_lam_im[l], ssm_log_dt[l], ssm_b_re[l], ssm_b_im[l],
                         ssm_c_re[l], ssm_c_im[l], ssm_d[l])
        lq1, lk1, lq2, lk2 = row(att_lq1), row(att_lk1), row(att_lq2), row(att_lk2)
        gn_att = row(att_norm)
        gn_hg = jnp.tile(row(hg_norm), (1, N_HEADS))
        lb = lb_all[l].reshape(1, -1)
        final = l == depth - 1
        gfin = norm_final.astype(F32).reshape(1, -1)

        mkv = mem_kv_proj(memp, row(mem_norm), w_mkv[l])
        z, zb, zg, k_rows, v_rows, vt = in_proj(hp, row(norm_mix), w_in_p[l], *tabs_p, tm_p, t_len=s)
        ys, sr, si = s5_mixer(z, zeros_p, zeros_p, s5p, w_glu[l], b, s, tt_p)
        oa = attn_prompt(cst, zb, vt.reshape(-1, s), lq1, lk1, lq2, lk2, gn_att, b, s, tq_p)
        oh, sh = hgrn_mixer(z, lb, gn_hg, zeros_hg, b, s, tt_p)
        om = mem_attn(jnp.zeros((1,), jnp.int32), z, mkv, mkv, 0, 1, b, s, tt_p)
        hp = merge_out(hp, ys, oa, oh, om, zg, wbs[l], wba[l], wbh[l], wbm[l], wo[l], 512)
        hp = ffn(hp, row(norm_ffn), wfi[l], wfo[l], gfin, 512, D_FF // 2, final)
        outs["kp"].append(k_rows.reshape(b, s, N_HEADS, HEAD_LANES))
        outs["vp"].append(v_rows.reshape(b, s, N_HEADS, HEAD_LANES))
        outs["mkp"].append(mkv[:, :memw].reshape(b, MEM_TOKENS, N_HEADS, memw // N_HEADS))
        outs["mvp"].append(mkv[:, memw:].reshape(b, MEM_TOKENS, N_HEADS, memw // N_HEADS))
        outs["srp"].append(sr.reshape(b, g_ssm, p_ssm))
        outs["sip"].append(si.reshape(b, g_ssm, p_ssm))
        outs["hgp"].append(_rows_to_state(sh, N_HEADS))

        z, zb, zg, k_rows, v_rows = in_proj(hs, row(norm_mix), w_in_p[l], *tabs_s, tm_s)
        ys, sr, si = s5_mixer(z, state_ssm_re[l].reshape(db, 1, -1), state_ssm_im[l].reshape(db, 1, -1),
                              s5p, w_glu[l], db, ds, ds)
        oa = attn_sample(page_table, lyr, cst, zb, ck, cv, lq1, lk1, lq2, lk2, gn_att, ds)
        oh, sh = hgrn_mixer(z, lb, gn_hg, _state_to_rows(state_hgrn[l]), db, ds, ds)
        om = mem_attn(jnp.full((1,), l * db // bps, jnp.int32), z, cmk, cmv, 0, 0, db, ds, ds,
                      kv_t=True, bps=bps)
        hs = merge_out(hs, ys, oa, oh, om, zg, wbs[l], wba[l], wbh[l], wbm[l], wo[l], 512)
        hs = ffn(hs, row(norm_ffn), wfi[l], wfo[l], gfin, 512, D_FF // 2, final)
        outs["ks"].append(k_rows.reshape(db, ds, N_HEADS, HEAD_LANES))
        outs["vs"].append(v_rows.reshape(db, ds, N_HEADS, HEAD_LANES))
        outs["srs"].append(sr.reshape(db, g_ssm, p_ssm))
        outs["sis"].append(si.reshape(db, g_ssm, p_ssm))
        outs["hgs"].append(_rows_to_state(sh, N_HEADS))

    st = jnp.stack
    return (hp.reshape(b, s, D_MODEL), hs.reshape(db, ds, D_MODEL),
            st(outs["kp"]), st(outs["vp"]), st(outs["mkp"]), st(outs["mvp"]),
            st(outs["srp"]), st(outs["sip"]), st(outs["hgp"]),
            st(outs["ks"]), st(outs["vs"]), st(outs["srs"]), st(outs["sis"]), st(outs["hgs"]))
```

```python
import functools
import math

import jax
import jax.numpy as jnp
from jax import lax
from jax.experimental import pallas as pl
from jax.experimental.pallas import tpu as pltpu

F32 = jnp.float32
BF16 = jnp.bfloat16
SDS = jax.ShapeDtypeStruct

D_MODEL = 1024
EPS = 1e-6
ROPE_THETA = 10000.0
MASK_VALUE = -1e30
HG_F_FLOOR = 1e-20
N_HEADS = 4
HEAD_LANES = 128
MEM_TOKENS = 256
D_FF = 2816
IN_WIDTH = 7168
IN_TILE = 512
CHUNK = 8
HG_CHUNK = 16
LOG2E = math.log2(math.e)
Q_SCALE = 64 ** -0.5 * math.log2(math.e)
VT_ROWS = HEAD_LANES + 16
VMEM_LIMIT = 56 * 1024 * 1024

Z_TILE0, Z_TILES = 3, 3
GATE_TILE0 = 6
COL_HQ, COL_HF, COL_HI, COL_HG, COL_MQ, COL_U = 0, 1, 2, 3, 4, 5


def _cparams(n_axes):
    return pltpu.CompilerParams(dimension_semantics=("arbitrary",) * n_axes,
                                vmem_limit_bytes=VMEM_LIMIT)


def _sigmoid(x):
    return 0.5 * jnp.tanh(0.5 * x) + 0.5


def _silu(x):
    return x * _sigmoid(x)


def _rms(x, g):
    return x * lax.rsqrt(jnp.mean(x * x, axis=-1, keepdims=True) + EPS) * g


def _dot(a, b):
    return jnp.dot(a, b, preferred_element_type=F32)


def _dot_nt(a, b):
    return lax.dot_general(a, b, (((1,), (1,)), ((), ())), preferred_element_type=F32)


def _block_ones(n, blk, dtype):
    r = lax.broadcasted_iota(jnp.int32, (n, n), 0) // blk
    c = lax.broadcasted_iota(jnp.int32, (n, n), 1) // blk
    return r == c


def _inproj_kernel(x_ref, g_ref, w0_ref, w1_ref, cos_ref, sa_ref, sb_ref, z_ref, zb_ref, zg_ref, kr_ref,
                   vr_ref, *rest):
    xn_ref, vt_ref = rest[-1], (rest[0] if len(rest) == 2 else None)
    j = pl.program_id(1)
    tm = x_ref.shape[0]
    half = IN_TILE // 2

    @pl.when(j == 0)
    def _():
        xn_ref[...] = _rms(x_ref[...], g_ref[...]).astype(BF16)

    def halves():
        for c in range(2):
            yield c, slice(c * half, (c + 1) * half), _dot(xn_ref[...], (w0_ref, w1_ref)[c][...])

    def rotary(scale, rows_ref):
        cos, sa, sb = cos_ref[...], sa_ref[...], sb_ref[...]
        for c, _, acc in halves():
            for hh in range(2 * c, 2 * c + 2):
                sl = slice(hh * HEAD_LANES, (hh + 1) * HEAD_LANES)
                x = acc[:, (hh - 2 * c) * HEAD_LANES:(hh - 2 * c + 1) * HEAD_LANES]
                r = (x * cos + pltpu.roll(x, 32, 1) * sa + pltpu.roll(x, 96, 1) * sb)
                zb_ref[:, sl] = (r * scale).astype(BF16) if scale is not None else r.astype(BF16)
                if rows_ref is not None:
                    rows_ref[pl.ds(hh, tm, stride=N_HEADS), :] = r

    @pl.when(j == 0)
    def _():
        rotary(Q_SCALE, None)

    @pl.when(j == 1)
    def _():
        rotary(None, kr_ref)

    @pl.when(j == 2)
    def _():
        for c, sl, acc in halves():
            zb_ref[:, sl] = acc.astype(BF16)
            for hh in range(2 * c, 2 * c + 2):
                vh = acc[:, (hh - 2 * c) * HEAD_LANES:(hh - 2 * c + 1) * HEAD_LANES]
                vr_ref[pl.ds(hh, tm, stride=N_HEADS), :] = vh
                if vt_ref is not None:
                    extra = lax.broadcasted_iota(jnp.int32, (VT_ROWS - HEAD_LANES, tm), 0) == 0
                    vt_ref[hh, 0:HEAD_LANES, :] = vh.T.astype(BF16)
                    vt_ref[hh, HEAD_LANES:, :] = jnp.where(extra, 1.0, 0.0).astype(BF16)

    @pl.when(jnp.logical_and(j > 2, j < 6))
    def _():
        for _, sl, acc in halves():
            z_ref[:, sl] = acc

    @pl.when(j >= 6)
    def _():
        for _, sl, acc in halves():
            zg_ref[:, sl] = _sigmoid(acc).astype(BF16)


def in_proj(h, g, w, layer, cos_t, sa_t, sb_t, tm, t_len=None):
    n = h.shape[0]
    half = IN_TILE // 2
    first = lambda i, j: (layer, 0, jnp.where(j < 6, 2 * j + 1, 2 * j))
    second = lambda i, j: (layer, 0, jnp.where(j < 5, 2 * j + 2, jnp.where(j == 5, 0, 2 * j + 1)))
    nper = cos_t.shape[0] // tm
    tab = pl.BlockSpec((tm, HEAD_LANES), lambda i, j: (i % nper, 0))
    rows = pl.BlockSpec((tm * N_HEADS, HEAD_LANES), lambda i, j: (i, 0))
    vt_specs, vt_shapes = [], []
    if t_len is not None:
        per_seq = t_len // tm
        vt_specs = [pl.BlockSpec((N_HEADS, VT_ROWS, tm), lambda i, j: (i // per_seq, 0, i % per_seq))]
        vt_shapes = [SDS((n // t_len * N_HEADS, VT_ROWS, t_len), BF16)]
    return pl.pallas_call(
        _inproj_kernel,
        grid=(n // tm, IN_WIDTH // IN_TILE),
        in_specs=[pl.BlockSpec((tm, D_MODEL), lambda i, j: (i, 0)),
                  pl.BlockSpec((1, D_MODEL), lambda i, j: (0, 0)),
                  pl.BlockSpec((None, D_MODEL, half), first),
                  pl.BlockSpec((None, D_MODEL, half), second),
                  tab, tab, tab],
        out_specs=[pl.BlockSpec((tm, IN_TILE), lambda i, j: (i, jnp.clip(j - Z_TILE0, 0, Z_TILES - 1))),
                   pl.BlockSpec((tm, IN_TILE), lambda i, j: (i, jnp.minimum(j, 2))),
                   pl.BlockSpec((tm, IN_TILE), lambda i, j: (i, jnp.maximum(j - GATE_TILE0, 0))),
                   rows, rows] + vt_specs,
        out_shape=[SDS((n, Z_TILES * IN_TILE), F32), SDS((n, 3 * IN_TILE), BF16),
                   SDS((n, IN_WIDTH - GATE_TILE0 * IN_TILE), BF16),
                   SDS((n * N_HEADS, HEAD_LANES), F32), SDS((n * N_HEADS, HEAD_LANES), F32)] + vt_shapes,
        scratch_shapes=[pltpu.VMEM((tm, D_MODEL), BF16)],
        compiler_params=_cparams(2),
        name="in_proj",
    )(h, g, w, w, cos_t, sa_t, sb_t)


def _memkv_kernel(x_ref, g_ref, w_ref, o_ref):
    o_ref[...] = _dot(_rms(x_ref[...], g_ref[...]).astype(BF16), w_ref[...])


def mem_kv_proj(x, g, w):
    n, wd = x.shape[0], w.shape[1]
    tm = 256
    return pl.pallas_call(
        _memkv_kernel,
        grid=(n // tm,),
        in_specs=[pl.BlockSpec((tm, D_MODEL), lambda i: (i, 0)),
                  pl.BlockSpec((1, D_MODEL), lambda i: (0, 0)),
                  pl.BlockSpec((D_MODEL, wd), lambda i: (0, 0))],
        out_specs=pl.BlockSpec((tm, wd), lambda i: (i, 0)),
        out_shape=SDS((n, wd), F32),
        compiler_params=_cparams(1),
        name="mem_kv_proj",
    )(x, g, w)


def _s5_kernel(u_ref, h0r_ref, h0i_ref, bre_ref, bim_ref, cre_ref, cim_ref, d_ref, wg_ref,
               pw_ref, ys_ref, sr_ref, si_ref, xr_sc, xi_sc, hr_sc, hi_sc, *, per_seq):
    t = pl.program_id(1)
    tt = u_ref.shape[0]
    ns = xr_sc.shape[1]
    n_chunks = tt // CHUNK
    u = u_ref[...]
    ub = u.astype(BF16)
    xr_sc[...] = _dot(ub, bre_ref[...])
    xi_sc[...] = _dot(ub, bim_ref[...])
    apr, api = pw_ref[6], pw_ref[7]

    def scan_chunk(c, hr, hi):
        sl = pl.ds(pl.multiple_of(c * CHUNK, CHUNK), CHUNK)
        xr, xi = xr_sc[sl, :], xi_sc[sl, :]
        for k, s in enumerate((1, 2, 4)):
            ar, ai = pw_ref[2 * k], pw_ref[2 * k + 1]
            pr, pi = pltpu.roll(xr, s, 0), pltpu.roll(xi, s, 0)
            xr, xi = xr + (ar * pr - ai * pi), xi + (ar * pi + ai * pr)
        xr, xi = xr + (apr * hr - api * hi), xi + (apr * hi + api * hr)
        xr_sc[sl, :] = xr
        xi_sc[sl, :] = xi
        return xr, xi

    if per_seq:
        def body(c, carry):
            one = pl.ds(c, 1)
            xr, xi = scan_chunk(c, jnp.broadcast_to(h0r_ref[one, :], (CHUNK, ns)),
                                jnp.broadcast_to(h0i_ref[one, :], (CHUNK, ns)))
            sr_ref[one, :] = xr[CHUNK - 1:CHUNK, :]
            si_ref[one, :] = xi[CHUNK - 1:CHUNK, :]
            return carry

        lax.fori_loop(0, n_chunks, body, 0, unroll=math.gcd(4, n_chunks))
    else:
        @pl.when(t == 0)
        def _():
            hr_sc[...] = jnp.broadcast_to(h0r_ref[...], hr_sc.shape)
            hi_sc[...] = jnp.broadcast_to(h0i_ref[...], hi_sc.shape)

        def body(c, carry):
            xr, xi = scan_chunk(c, *carry)
            return (jnp.broadcast_to(xr[CHUNK - 1:CHUNK, :], xr.shape),
                    jnp.broadcast_to(xi[CHUNK - 1:CHUNK, :], xi.shape))

        hr, hi = lax.fori_loop(0, n_chunks, body, (hr_sc[...], hi_sc[...]),
                               unroll=math.gcd(4, n_chunks))
        hr_sc[...] = hr
        hi_sc[...] = hi
        sr_ref[...] = hr[0:1, :]
        si_ref[...] = hi[0:1, :]

    y = (_dot(xr_sc[...].astype(BF16), cre_ref[...]) - _dot(xi_sc[...].astype(BF16), cim_ref[...])
         + d_ref[...] * u)
    ys = jax.nn.gelu(y)
    gate = _sigmoid(_dot(ys.astype(BF16), wg_ref[...]))
    ys_ref[...] = (ys * gate).astype(BF16)


def s5_mixer(z, h0r, h0i, prm, wglu, nb, t_len, tt):
    per_seq = t_len == CHUNK
    ns = h0r.shape[-1]
    width = prm["d"].shape[-1]
    full = lambda a: pl.BlockSpec(a.shape, lambda b, t: (0,) * a.ndim)
    if per_seq:
        grid, tt = (1, 1), nb * t_len
        h0r, h0i = h0r.reshape(nb, ns), h0i.reshape(nb, ns)
        st = pl.BlockSpec((nb, ns), lambda b, t: (0, 0))
        rows = lambda cidx: pl.BlockSpec((tt, width), lambda b, t: (0, cidx))
    else:
        nt = t_len // tt
        grid = (nb, nt)
        st = pl.BlockSpec((None, 1, ns), lambda b, t: (b, 0, 0))
        rows = lambda cidx: pl.BlockSpec((tt, width), lambda b, t: (b * nt + t, cidx))
    return pl.pallas_call(
        functools.partial(_s5_kernel, per_seq=per_seq),
        grid=grid,
        in_specs=[rows(COL_U), st, st,
                  full(prm["bre"]), full(prm["bim"]), full(prm["cre"]), full(prm["cim"]),
                  full(prm["d"]), full(wglu), full(prm["pw"])],
        out_specs=[rows(0), st, st],
        out_shape=[SDS((nb * t_len, width), BF16), SDS(h0r.shape, F32), SDS(h0r.shape, F32)],
        scratch_shapes=[pltpu.VMEM((tt, ns), F32), pltpu.VMEM((tt, ns), F32),
                        pltpu.VMEM((CHUNK, ns), F32), pltpu.VMEM((CHUNK, ns), F32)],
        compiler_params=_cparams(2),
        name="s5",
    )(z, h0r, h0i, prm["bre"], prm["bim"], prm["cre"], prm["cim"], prm["d"], wglu, prm["pw"])


def _lambda(lq1, lk1, lq2, lk2, lam_init):
    return (jnp.exp(jnp.sum(lq1[...] * lk1[...], axis=-1, keepdims=True))
            - jnp.exp(jnp.sum(lq2[...] * lk2[...], axis=-1, keepdims=True)) + lam_init)


def _two_maps(q):
    lane = lax.broadcasted_iota(jnp.int32, (1, HEAD_LANES), 1)
    zero = jnp.zeros_like(q)
    return jnp.concatenate([jnp.where(lane < 64, q, zero), jnp.where(lane >= 64, q, zero)], axis=0)


def _diff_finish(o1, o2, lam, lam_init, gn):
    return _rms(o1 - lam * o2, gn) * (1.0 - lam_init)


def _attn_prompt_kernel(cst_ref, q_ref, k_ref, vt_ref, lq1, lk1, lq2, lk2, gn_ref, o_ref,
                        m_sc, acc_sc, s0_sc, s1_sc, b0_sc, b1_sc):
    qi = pl.program_id(2)
    tq = q_ref.shape[0]
    w = 2 * tq
    lam_init = cst_ref[0]
    qq = _two_maps(q_ref[...])
    m_sc[...] = jnp.full(m_sc.shape, MASK_VALUE, F32)
    acc_sc[...] = jnp.zeros(acc_sc.shape, F32)
    slots = ((s0_sc, b0_sc), (s1_sc, b1_sc))

    def scores(kb, slot):
        sl = pl.ds(pl.multiple_of(kb * tq, tq), tq)
        s = _dot_nt(k_ref[sl, :], qq)
        slots[slot][0][...] = s
        slots[slot][1][...] = jnp.max(s, axis=0, keepdims=True)

    def consume(kb, slot, masked):
        sl = pl.ds(pl.multiple_of(kb * tq, tq), tq)
        s = slots[slot][0][...]
        if masked:
            key = lax.broadcasted_iota(jnp.int32, (tq, w), 0)
            qry = lax.broadcasted_iota(jnp.int32, (tq, w), 1) % tq
            s = jnp.where(key <= qry, s, MASK_VALUE)
            bmax = jnp.max(s, axis=0, keepdims=True)
        else:
            bmax = slots[slot][1][...]
        m_prev = m_sc[...]
        m_new = jnp.maximum(m_prev, bmax)
        p = jnp.exp2(s - m_new)
        acc_sc[...] = jnp.exp2(m_prev - m_new) * acc_sc[...] + _dot(vt_ref[:, sl], p.astype(BF16))
        m_sc[...] = m_new

    def body(j, carry):
        scores(2 * j + 1, 1)
        consume(2 * j, 0, False)
        scores(2 * j + 2, 0)
        consume(2 * j + 1, 1, False)
        return carry

    scores(0, 0)
    lax.fori_loop(0, qi // 2, body, 0)

    @pl.when(qi % 2 == 0)
    def _():
        consume(qi, 0, True)

    @pl.when(qi % 2 == 1)
    def _():
        scores(qi, 1)
        consume(qi - 1, 0, False)
        consume(qi, 1, True)

    lam = _lambda(lq1, lk1, lq2, lk2, lam_init)
    acc = acc_sc[...]
    ot = acc[:HEAD_LANES, :] / acc[HEAD_LANES:HEAD_LANES + 1, :]
    o_ref[...] = _diff_finish(ot[:, :tq].T, ot[:, tq:].T, lam, lam_init, gn_ref[...]).astype(BF16)


def attn_prompt(cst, zb, vt, lq1, lk1, lq2, lk2, gn, nb, t_len, tq):
    nq = t_len // tq
    vec = pl.BlockSpec((1, 64), lambda b, h, i: (0, 0))
    return pl.pallas_call(
        _attn_prompt_kernel,
        grid=(nb, N_HEADS, nq),
        in_specs=[pl.BlockSpec(memory_space=pltpu.SMEM),
                  pl.BlockSpec((tq, HEAD_LANES), lambda b, h, i: (b * nq + i, h)),
                  pl.BlockSpec((t_len, HEAD_LANES), lambda b, h, i: (b, N_HEADS + h)),
                  pl.BlockSpec((VT_ROWS, t_len), lambda b, h, i: (b * N_HEADS + h, 0)),
                  vec, vec, vec, vec,
                  pl.BlockSpec((1, HEAD_LANES), lambda b, h, i: (0, 0))],
        out_specs=pl.BlockSpec((tq, HEAD_LANES), lambda b, h, i: (b * nq + i, h)),
        out_shape=SDS((nb * t_len, N_HEADS * HEAD_LANES), BF16),
        scratch_shapes=[pltpu.VMEM((1, 2 * tq), F32), pltpu.VMEM((VT_ROWS, 2 * tq), F32),
                        pltpu.VMEM((tq, 2 * tq), F32), pltpu.VMEM((tq, 2 * tq), F32),
                        pltpu.VMEM((1, 2 * tq), F32), pltpu.VMEM((1, 2 * tq), F32)],
        compiler_params=_cparams(3),
        name="attn_prompt",
    )(cst, zb, zb, vt, lq1, lk1, lq2, lk2, gn)


def _attn_sample_kernel(pt_ref, lyr_ref, cst_ref, q_ref, kn_ref, vn_ref, lq1, lk1, lq2, lk2, gn_ref,
                        *rest):
    n_pages = (len(rest) - 1) // 2
    k_refs, v_refs, o_ref = rest[:n_pages], rest[n_pages:2 * n_pages], rest[-1]
    t = q_ref.shape[0]
    lam_init = cst_ref[0]
    lam = _lambda(lq1, lk1, lq2, lk2, lam_init)
    hsl = [slice(hh * HEAD_LANES, (hh + 1) * HEAD_LANES) for hh in range(N_HEADS)]
    qh = [_two_maps(q_ref[:, sl]) for sl in hsl]
    qq = jnp.concatenate(qh, axis=0)
    rows, cols = N_HEADS * 2 * t, k_refs[0].shape[0]
    same_head = (lax.broadcasted_iota(jnp.int32, (rows, cols), 0) // (2 * t)
                 == lax.broadcasted_iota(jnp.int32, (rows, cols), 1) % N_HEADS)
    s_past = [jnp.where(same_head, _dot_nt(qq, k_refs[p][...].astype(BF16)), MASK_VALUE)
              for p in range(n_pages)]
    r = lax.broadcasted_iota(jnp.int32, (2 * t, t), 0) % t
    c = lax.broadcasted_iota(jnp.int32, (2 * t, t), 1)
    s_new = jnp.concatenate([jnp.where(c <= r, _dot_nt(qh[hh], kn_ref[:, hsl[hh]]), MASK_VALUE)
                             for hh in range(N_HEADS)], axis=0)
    m = s_past[0]
    for sp in s_past[1:]:
        m = jnp.maximum(m, sp)
    m = jnp.maximum(jnp.max(m, axis=-1, keepdims=True), jnp.max(s_new, axis=-1, keepdims=True))
    p_new = jnp.exp2(s_new - m)
    l = jnp.sum(p_new, axis=-1, keepdims=True)
    acc = jnp.concatenate([_dot(p_new[hh * 2 * t:(hh + 1) * 2 * t].astype(BF16), vn_ref[:, hsl[hh]])
                           for hh in range(N_HEADS)], axis=0)
    lsum = jnp.zeros(s_past[0].shape, F32)
    for p in range(n_pages):
        pp = jnp.exp2(s_past[p] - m)
        lsum = lsum + pp
        acc = acc + _dot(pp.astype(BF16), v_refs[p][...].astype(BF16))
    o = acc / (l + jnp.sum(lsum, axis=-1, keepdims=True))
    for hh in range(N_HEADS):
        o1 = o[hh * 2 * t:hh * 2 * t + t]
        o2 = o[hh * 2 * t + t:(hh + 1) * 2 * t]
        o_ref[:, hsl[hh]] = _diff_finish(o1, o2, lam, lam_init, gn_ref[...]).astype(BF16)


def attn_sample(page_table, lyr, cst, zb, cache_k, cache_v, lq1, lk1, lq2, lk2, gn, t_len):
    nb, n_pages = page_table.shape
    page_rows = cache_k.shape[2]
    width = N_HEADS * HEAD_LANES
    pt = page_table.reshape(-1)
    vec = pl.BlockSpec((1, 64), lambda b, pt, ly: (0, 0))

    def page_spec(p):
        return pl.BlockSpec((None, None, page_rows, HEAD_LANES),
                            lambda b, pt, ly: (ly[0], pt[b * n_pages + p], 0, 0))

    pages = [page_spec(p) for p in range(n_pages)]
    grid_spec = pltpu.PrefetchScalarGridSpec(
        num_scalar_prefetch=2,
        grid=(nb,),
        in_specs=[pl.BlockSpec(memory_space=pltpu.SMEM),
                  pl.BlockSpec((t_len, width), lambda b, pt, ly: (b, 0)),
                  pl.BlockSpec((t_len, width), lambda b, pt, ly: (b, 1)),
                  pl.BlockSpec((t_len, width), lambda b, pt, ly: (b, 2)),
                  vec, vec, vec, vec,
                  pl.BlockSpec((1, HEAD_LANES), lambda b, pt, ly: (0, 0))] + pages + pages,
        out_specs=pl.BlockSpec((t_len, width), lambda b, pt, ly: (b, 0)),
    )
    return pl.pallas_call(
        _attn_sample_kernel,
        grid_spec=grid_spec,
        out_shape=SDS((nb * t_len, width), BF16),
        compiler_params=_cparams(1),
        name="attn_sample",
    )(pt, lyr, cst, zb, zb, zb, lq1, lk1, lq2, lk2, gn,
      *([cache_k] * n_pages), *([cache_v] * n_pages))


def _hgrn_kernel(hq_ref, hf_ref, hi_ref, hg_ref, lb_ref, gn_ref, s0_ref, oh_ref, sout_ref,
                 s_sc, u_sc, hist_sc, dec_sc, o_sc, *, per_seq, chunk):
    t = pl.program_id(1)
    nt = pl.num_programs(1)
    tt, w = hq_ref.shape
    dk = w // N_HEADS
    n = tt // chunk
    blk = _block_ones(w, dk, F32)
    ones_bf = jnp.where(blk, 1.0, 0.0).astype(BF16)

    def expand(s0):
        return jnp.where(blk, jnp.concatenate([s0] * N_HEADS, axis=1), 0.0)

    def contract(sf):
        acc = sf[:, 0:dk]
        for hh in range(1, N_HEADS):
            acc = acc + sf[:, hh * dk:(hh + 1) * dk]
        return acc

    if not per_seq:
        @pl.when(t == 0)
        def _():
            s_sc[...] = expand(s0_ref[...])

    lb = lb_ref[...]
    f = lb + (1.0 - lb) / (1.0 + jnp.exp(-hf_ref[...]))
    k = 1.0 - f
    q = _silu(hq_ref[...])
    v = hi_ref[...]
    row = lax.broadcasted_iota(jnp.int32, (tt, 1), 0) % chunk
    cum = jnp.log(jnp.maximum(f, HG_F_FLOOR))
    s = 1
    while s < chunk:
        cum = cum + jnp.where(row >= s, pltpu.roll(cum, s, 0), 0.0)
        s *= 2
    cum2 = cum * LOG2E

    def chunk_row(x, s):
        x3 = x.reshape(n, chunk, w)
        return jnp.broadcast_to(x3[:, s:s + 1, :], (n, chunk, w)).reshape(tt, w)

    def tail(x, j):
        rows = chunk - j * CHUNK
        return x.reshape(n, chunk, x.shape[-1])[:, j * CHUNK:, :].reshape(n * rows, x.shape[-1])

    groups = chunk // CHUNK
    o_parts = [None] * groups
    for s in range(chunk):
        j = s // CHUNK
        keep = tail(row, j) >= s
        e = jnp.where(keep, jnp.exp2(jnp.minimum(tail(cum2, j) - tail(chunk_row(cum2, s), j), 0.0)), 0.0)
        att = _dot((tail(q, j) * e * tail(chunk_row(k, s), j)).astype(BF16), ones_bf)
        term = att * tail(chunk_row(v, s), j)
        o_parts[j] = term if o_parts[j] is None else o_parts[j] + term
    o3 = None
    for j in range(groups):
        pj = o_parts[j].reshape(n, chunk - j * CHUNK, w)
        if j:
            pj = jnp.concatenate([jnp.zeros((n, j * CHUNK, w), F32), pj], axis=1)
        o3 = pj if o3 is None else o3 + pj
    o_sc[...] = o3.reshape(tt, w)
    last = chunk_row(cum, chunk - 1)
    qd = (q * jnp.exp(cum)).astype(BF16)
    dec_sc[...] = jnp.exp(last)

    vb = v.astype(BF16)
    kb = (k * jnp.exp(last - cum)).astype(BF16)
    for c in range(n):
        sl = slice(c * chunk, (c + 1) * chunk)
        lhs, rhs = (vb[sl], kb[sl]) if chunk >= 16 else (vb[sl].astype(F32), kb[sl].astype(F32))
        upd = lax.dot_general(lhs, rhs, (((0,), (0,)), ((), ())), preferred_element_type=F32)
        u_sc[c] = jnp.where(blk, upd, 0.0)

    if per_seq:
        for c in range(n):
            s_old = expand(s0_ref[c])
            hist_sc[c] = s_old.astype(BF16)
            sout_ref[c] = contract(dec_sc[c * chunk:c * chunk + 1, :] * s_old + u_sc[c])
    else:
        def body(c, s_old):
            hist_sc[c] = s_old.astype(BF16)
            return dec_sc[pl.ds(pl.multiple_of(c * chunk, chunk), 1), :] * s_old + u_sc[c]

        s_sc[...] = lax.fori_loop(0, n, body, s_sc[...], unroll=math.gcd(4, n))

    for c in range(n):
        sl = slice(c * chunk, (c + 1) * chunk)
        o_sc[sl, :] = o_sc[sl, :] + _dot_nt(qd[sl], hist_sc[c])

    o = o_sc[...]
    o2 = o * o
    hi = o2.astype(BF16)
    lo = (o2 - hi.astype(F32)).astype(BF16)
    ms = (_dot(hi, ones_bf) + _dot(lo, ones_bf)) * (1.0 / dk)
    on = o * lax.rsqrt(ms + EPS) * gn_ref[...]
    oh_ref[...] = (on * _silu(hg_ref[...])).astype(BF16)

    if not per_seq:
        @pl.when(t == nt - 1)
        def _():
            sout_ref[...] = contract(s_sc[...])


def hgrn_mixer(z, lb, gn, s0, nb, t_len, tt):
    per_seq = t_len == CHUNK
    w = lb.shape[-1]
    dk = w // N_HEADS
    row = pl.BlockSpec((1, w), lambda b, t: (0, 0))
    if per_seq:
        sps = math.gcd(nb, 16)
        nt, tt, grid, chunk = 1, sps * t_len, (nb // sps, 1), t_len
        st = pl.BlockSpec((sps, w, dk), lambda b, t: (b, 0, 0))
    else:
        nt = t_len // tt
        grid = (nb, nt)
        chunk = math.gcd(HG_CHUNK, tt)
        st = pl.BlockSpec((None, w, dk), lambda b, t: (b, 0, 0))
    col = lambda cidx: pl.BlockSpec((tt, w), lambda b, t: (b * nt + t, cidx))
    return pl.pallas_call(
        functools.partial(_hgrn_kernel, per_seq=per_seq, chunk=chunk),
        grid=grid,
        in_specs=[col(COL_HQ), col(COL_HF), col(COL_HI), col(COL_HG), row, row, st],
        out_specs=[pl.BlockSpec((tt, w), lambda b, t: (b * nt + t, 0)), st],
        out_shape=[SDS((nb * t_len, w), BF16), SDS((nb, w, dk), F32)],
        scratch_shapes=[pltpu.VMEM((w, w), F32), pltpu.VMEM((tt // chunk, w, w), F32),
                        pltpu.VMEM((tt // chunk, w, w), BF16),
                        pltpu.VMEM((tt, w), F32), pltpu.VMEM((tt, w), F32)],
        compiler_params=_cparams(2),
        name="hgrn",
    )(z, z, z, z, lb, gn, s0)


def _mem_attn_kernel(off_ref, q_ref, mk_ref, mv_ref, o_ref, *, kv_t, bps):
    w = q_ref.shape[1]
    tq = q_ref.shape[0] // bps
    dh = w // N_HEADS
    head = lax.broadcasted_iota(jnp.int32, (1, w), 1) // dh
    for e in range(bps):
        q = q_ref[e * tq:(e + 1) * tq, :].astype(BF16) * jnp.asarray(dh ** -0.5, BF16)
        mk = mk_ref[e * w:(e + 1) * w, :].astype(BF16)
        mv = mv_ref[e * w:(e + 1) * w, :].astype(BF16)
        out = jnp.zeros((tq, w), F32)
        for hh in range(N_HEADS):
            qh = jnp.where(head == hh, q, jnp.zeros_like(q))
            s = _dot(qh, mk) if kv_t else _dot_nt(qh, mk)
            p = jnp.exp(s - jnp.max(s, axis=-1, keepdims=True))
            p = (p / jnp.sum(p, axis=-1, keepdims=True)).astype(BF16)
            out = out + jnp.where(head == hh, _dot_nt(p, mv) if kv_t else _dot(p, mv), 0.0)
        o_ref[e * tq:(e + 1) * tq, :] = out.astype(BF16)


def mem_attn(off, z, mk, mv, k_col, v_col, nb, t_len, tq, kv_t=False, bps=1):
    nq = t_len // tq
    w = MEM_TOKENS
    grid_spec = pltpu.PrefetchScalarGridSpec(
        num_scalar_prefetch=1,
        grid=(nb // bps, nq),
        in_specs=[pl.BlockSpec((bps * tq, w), lambda b, i, off: (b * nq + i, COL_MQ)),
                  pl.BlockSpec((bps * MEM_TOKENS, w), lambda b, i, off: (off[0] + b, k_col)),
                  pl.BlockSpec((bps * MEM_TOKENS, w), lambda b, i, off: (off[0] + b, v_col))],
        out_specs=pl.BlockSpec((bps * tq, w), lambda b, i, off: (b * nq + i, 0)),
    )
    return pl.pallas_call(
        functools.partial(_mem_attn_kernel, kv_t=kv_t, bps=bps),
        grid_spec=grid_spec,
        out_shape=SDS((nb * t_len, w), BF16),
        compiler_params=_cparams(2),
        name="mem_attn",
    )(off, z, mk, mv)


def _merge_kernel(h_ref, ys_ref, oa_ref, oh_ref, om_ref, g0, g1, g2, g3, ws, wa, wh, wm, wo, o_ref):
    m = (g0[...] * _dot(ys_ref[...], ws[...]) + g1[...] * _dot(oa_ref[...], wa[...])
         + g2[...] * _dot(oh_ref[...], wh[...]) + g3[...] * _dot(om_ref[...], wm[...]))
    o_ref[...] = h_ref[...] + _dot(m.astype(BF16), wo[...])


def merge_out(h, ys, oa, oh, om, z, ws, wa, wh, wm, wo, tm):
    n = h.shape[0]
    tm = min(tm, n)
    rows = lambda a: pl.BlockSpec((tm, a.shape[1]), lambda i: (i, 0))
    full = lambda a: pl.BlockSpec(a.shape, lambda i: (0, 0))
    gate = lambda k: pl.BlockSpec((tm, D_MODEL), lambda i: (i, k))
    return pl.pallas_call(
        _merge_kernel,
        grid=(n // tm,),
        in_specs=[rows(h), rows(ys), rows(oa), rows(oh), rows(om),
                  gate(0), gate(1), gate(2), gate(3),
                  full(ws), full(wa), full(wh), full(wm), full(wo)],
        out_specs=rows(h),
        out_shape=SDS(h.shape, F32),
        compiler_params=_cparams(1),
        name="merge_out",
    )(h, ys, oa, oh, om, z, z, z, z, ws, wa, wh, wm, wo)


def _ffn_kernel(h_ref, g_ref, wg_ref, wu_ref, wo_ref, gf_ref, o_ref, hn_sc, acc_sc, *, final):
    j = pl.program_id(1)
    nj = pl.num_programs(1)

    @pl.when(j == 0)
    def _():
        hn_sc[...] = _rms(h_ref[...], g_ref[...]).astype(BF16)
        acc_sc[...] = jnp.zeros(acc_sc.shape, F32)

    hn = hn_sc[...]
    tf = wg_ref.shape[1]
    split = (tf // HEAD_LANES + 1) // 2 * HEAD_LANES
    acc = acc_sc[...]
    for lo, hi in ((0, split), (split, tf)):
        act = _silu(_dot(hn, wg_ref[:, lo:hi])) * _dot(hn, wu_ref[:, lo:hi])
        acc = acc + _dot(act.astype(BF16), wo_ref[lo:hi, :])
    acc_sc[...] = acc

    @pl.when(j == nj - 1)
    def _():
        out = h_ref[...] + acc_sc[...]
        o_ref[...] = _rms(out, gf_ref[...]) if final else out


def ffn(h, g, w_in, w_out, gf, tm, tf, final):
    n = h.shape[0]
    tm = min(tm, n)
    nf = D_FF // tf
    return pl.pallas_call(
        functools.partial(_ffn_kernel, final=final),
        grid=(n // tm, nf),
        in_specs=[pl.BlockSpec((tm, D_MODEL), lambda i, j: (i, 0)),
                  pl.BlockSpec((1, D_MODEL), lambda i, j: (0, 0)),
                  pl.BlockSpec((D_MODEL, tf), lambda i, j: (0, j)),
                  pl.BlockSpec((D_MODEL, tf), lambda i, j: (0, nf + j)),
                  pl.BlockSpec((tf, D_MODEL), lambda i, j: (j, 0)),
                  pl.BlockSpec((1, D_MODEL), lambda i, j: (0, 0))],
        out_specs=pl.BlockSpec((tm, D_MODEL), lambda i, j: (i, 0)),
        out_shape=SDS(h.shape, F32),
        scratch_shapes=[pltpu.VMEM((tm, D_MODEL), BF16), pltpu.VMEM((tm, D_MODEL), F32)],
        compiler_params=_cparams(2),
        name="ffn",
    )(h, g, w_in, w_in, w_out, gf)


def _rope_tables(pos):
    inv = ROPE_THETA ** (-jnp.arange(0, 64, 2, dtype=F32) / 64)
    ang = pos[:, None] * inv[None, :]
    ang = jnp.concatenate([ang, ang, ang, ang], axis=-1)
    cos, sin = jnp.cos(ang), jnp.sin(ang)
    upper = (jnp.arange(HEAD_LANES) % 64) >= 32
    return cos, jnp.where(upper, sin, 0.0), jnp.where(upper, 0.0, -sin)


def _block_diag(a):
    g, r, c = a.shape
    eye = jnp.eye(g, dtype=a.dtype)
    return (a[:, :, None, :] * eye[:, None, :, None]).reshape(g * r, g * c)


def _s5_params(lam_re, lam_im, log_dt, b_re, b_im, c_re, c_im, d):
    dt = jnp.exp(log_dt.astype(F32))[:, None]
    lr, li = lam_re.astype(F32), lam_im.astype(F32)
    mag = jnp.exp(lr * dt)
    ar, ai = mag * jnp.cos(li * dt), mag * jnp.sin(li * dt)
    nr, ni = ar - 1.0, ai
    den = lr * lr + li * li
    cr, ci = (nr * lr + ni * li) / den, (ni * lr - nr * li) / den
    br, bi = b_re.astype(F32), b_im.astype(F32)
    bbr = cr[..., None] * br - ci[..., None] * bi
    bbi = cr[..., None] * bi + ci[..., None] * br
    ar, ai = ar.reshape(1, -1), ai.reshape(1, -1)
    pows_r, pows_i = [ar], [ai]
    for _ in range(CHUNK - 1):
        pr, pi = pows_r[-1], pows_i[-1]
        pows_r.append(pr * ar - pi * ai)
        pows_i.append(pr * ai + pi * ar)
    rep = lambda x, s: jnp.where(jnp.arange(CHUNK)[:, None] >= s, x, 0.0)
    pw = jnp.stack([rep(pows_r[0], 1), rep(pows_i[0], 1), rep(pows_r[1], 2), rep(pows_i[1], 2),
                    rep(pows_r[3], 4), rep(pows_i[3], 4),
                    jnp.concatenate(pows_r, axis=0), jnp.concatenate(pows_i, axis=0)])
    return {
        "bre": _block_diag(jnp.swapaxes(bbr, 1, 2)).astype(BF16),
        "bim": _block_diag(jnp.swapaxes(bbi, 1, 2)).astype(BF16),
        "cre": _block_diag(jnp.swapaxes(c_re.astype(F32), 1, 2)).astype(BF16),
        "cim": _block_diag(jnp.swapaxes(c_im.astype(F32), 1, 2)).astype(BF16),
        "d": d.astype(F32).reshape(1, -1),
        "pw": pw,
    }


def _state_to_rows(s):
    b, h, dk, dv = s.shape
    return jnp.swapaxes(s, 2, 3).reshape(b, h * dv, dk)


def _rows_to_state(s, h):
    b, hv, dk = s.shape
    return jnp.swapaxes(s.reshape(b, h, hv // h, dk), 2, 3)


def kernel(x_prompt, x_sample, mem_prompt, cache_k, cache_v, page_table, cache_mem_k, cache_mem_v, state_ssm_re, state_ssm_im, state_hgrn, norm_mix, w_in, ssm_lam_re, ssm_lam_im, ssm_log_dt, ssm_b_re, ssm_b_im, ssm_c_re, ssm_c_im, ssm_d, ssm_w_glu, att_lq1, att_lk1, att_lq2, att_lk2, att_norm, hg_gamma, hg_norm, mem_norm, w_mem_kv, wb_ssm, wb_att, wb_hg, wb_mem, w_out, norm_ffn, w_ffn_in, w_ffn_out, norm_final):
    depth = w_in.shape[0]
    b, s, _ = x_prompt.shape
    db, ds, _ = x_sample.shape
    n_pages, page = page_table.shape[1], cache_k.shape[2]
    past = n_pages * page
    n_phys = cache_k.shape[1]
    g_ssm, p_ssm = state_ssm_re.shape[2], state_ssm_re.shape[3]
    hgw = hg_gamma.shape[1]
    memw = w_mem_kv.shape[2] // 2

    bf = lambda a: a.astype(BF16)
    w_in_b = bf(w_in)
    w_glu, w_mkv = bf(ssm_w_glu), bf(w_mem_kv)
    wbs, wba, wbh, wbm, wo = bf(wb_ssm), bf(wb_att), bf(wb_hg), bf(wb_mem), bf(w_out)
    wfi, wfo = bf(w_ffn_in), bf(w_ffn_out)
    gam = jax.nn.softmax(hg_gamma.astype(F32), axis=0)
    lb_all = jnp.cumsum(gam, axis=0) - gam[0]

    tabs_p = _rope_tables(jnp.arange(s, dtype=F32))
    tabs_s = tuple(jnp.tile(t, (db, 1)) for t in _rope_tables(past + jnp.arange(ds, dtype=F32)))

    ck = cache_k.reshape(depth, n_phys, page * N_HEADS, HEAD_LANES)
    cv = cache_v.reshape(depth, n_phys, page * N_HEADS, HEAD_LANES)
    cmk = jnp.transpose(cache_mem_k, (0, 1, 3, 4, 2)).reshape(depth * db * memw, MEM_TOKENS)
    cmv = jnp.transpose(cache_mem_v, (0, 1, 3, 4, 2)).reshape(depth * db * memw, MEM_TOKENS)
    bps = math.gcd(db, 8)
    memp = mem_prompt.reshape(b * MEM_TOKENS, D_MODEL)

    hp = x_prompt.reshape(b * s, D_MODEL)
    hs = x_sample.reshape(db * ds, D_MODEL)
    zeros_p = jnp.zeros((b, 1, g_ssm * p_ssm), F32)
    zeros_hg = jnp.zeros((b, hgw, hgw // N_HEADS), F32)

    tm_p = min(1024, s)
    tm_s = min(1024, db * ds)
    tt_p = min(512, s)
    tq_p = min(512, s)

    outs = {k: [] for k in ("kp", "vp", "mkp", "mvp", "srp", "sip", "hgp", "ks", "vs", "srs", "sis", "hgs")}
    for l in range(depth):
        lam_init = 0.8 - 0.6 * math.exp(-0.3 * l)
        cst = jnp.full((1,), lam_init, F32)
        lyr = jnp.full((1,), l, jnp.int32)
        row = lambda a: a[l].astype(F32).reshape(1, -1)
        s5p = _s5_params(ssm_lam_re[l], ssm_lam_im[l], ssm_log_dt[l], ssm_b_re[l], ssm_b_im[l],
                         ssm_c_re[l], ssm_c_im[l], ssm_d[l])
        lq1, lk1, lq2, lk2 = row(att_lq1), row(att_lk1), row(att_lq2), row(att_lk2)
        gn_att = row(att_norm)
        gn_hg = jnp.tile(row(hg_norm), (1, N_HEADS))
        lb = lb_all[l].reshape(1, -1)
        final = l == depth - 1
        gfin = norm_final.astype(F32).reshape(1, -1)

        mkv = mem_kv_proj(memp, row(mem_norm), w_mkv[l])
        z, zb, zg, k_rows, v_rows, vt = in_proj(hp, row(norm_mix), w_in_b, l, *tabs_p, tm_p, t_len=s)
        ys, sr, si = s5_mixer(z, zeros_p, zeros_p, s5p, w_glu[l], b, s, tt_p)
        oa = attn_prompt(cst, zb, vt.reshape(-1, s), lq1, lk1, lq2, lk2, gn_att, b, s, tq_p)
        oh, sh = hgrn_mixer(z, lb, gn_hg, zeros_hg, b, s, tt_p)
        om = mem_attn(jnp.zeros((1,), jnp.int32), z, mkv, mkv, 0, 1, b, s, tt_p)
        hp = merge_out(hp, ys, oa, oh, om, zg, wbs[l], wba[l], wbh[l], wbm[l], wo[l], 512)
        hp = ffn(hp, row(norm_ffn), wfi[l], wfo[l], gfin, 512, D_FF // 2, final)
        outs["kp"].append(k_rows.reshape(b, s, N_HEADS, HEAD_LANES))
        outs["vp"].append(v_rows.reshape(b, s, N_HEADS, HEAD_LANES))
        outs["mkp"].append(mkv[:, :memw].reshape(b, MEM_TOKENS, N_HEADS, memw // N_HEADS))
        outs["mvp"].append(mkv[:, memw:].reshape(b, MEM_TOKENS, N_HEADS, memw // N_HEADS))
        outs["srp"].append(sr.reshape(b, g_ssm, p_ssm))
        outs["sip"].append(si.reshape(b, g_ssm, p_ssm))
        outs["hgp"].append(_rows_to_state(sh, N_HEADS))

        z, zb, zg, k_rows, v_rows = in_proj(hs, row(norm_mix), w_in_b, l, *tabs_s, tm_s)
        ys, sr, si = s5_mixer(z, state_ssm_re[l].reshape(db, 1, -1), state_ssm_im[l].reshape(db, 1, -1),
                              s5p, w_glu[l], db, ds, ds)
        oa = attn_sample(page_table, lyr, cst, zb, ck, cv, lq1, lk1, lq2, lk2, gn_att, ds)
        oh, sh = hgrn_mixer(z, lb, gn_hg, _state_to_rows(state_hgrn[l]), db, ds, ds)
        om = mem_attn(jnp.full((1,), l * db // bps, jnp.int32), z, cmk, cmv, 0, 0, db, ds, ds,
                      kv_t=True, bps=bps)
        hs = merge_out(hs, ys, oa, oh, om, zg, wbs[l], wba[l], wbh[l], wbm[l], wo[l], 512)
        hs = ffn(hs, row(norm_ffn), wfi[l], wfo[l], gfin, 512, D_FF // 2, final)
        outs["ks"].append(k_rows.reshape(db, ds, N_HEADS, HEAD_LANES))
        outs["vs"].append(v_rows.reshape(db, ds, N_HEADS, HEAD_LANES))
        outs["srs"].append(sr.reshape(db, g_ssm, p_ssm))
        outs["sis"].append(si.reshape(db, g_ssm, p_ssm))
        outs["hgs"].append(_rows_to_state(sh, N_HEADS))

    st = jnp.stack
    return (hp.reshape(b, s, D_MODEL), hs.reshape(db, ds, D_MODEL),
            st(outs["kp"]), st(outs["vp"]), st(outs["mkp"]), st(outs["mvp"]),
            st(outs["srp"]), st(outs["sip"]), st(outs["hgp"]),
            st(outs["ks"]), st(outs["vs"]), st(outs["srs"]), st(outs["sis"]), st(outs["hgs"]))
```
